```python
import jax, jax.numpy as jnp
from jax import lax
import numpy as np

D_MODEL = 1024
BATCH = 4
SEQ = 4096
DEPTH = 2
DEC_BATCH = 32
DEC_SEQ = 4
PAST_LEN = 8192
PAGE_SIZE = 128

N_A = DEPTH // 2
N_B = DEPTH - N_A
POOL_WINDOWS = (2, 4, 8, 16)
N_POOL_GROUPS = len(POOL_WINDOWS)
W_A = 2 * D_MODEL
GW = W_A // N_POOL_GROUPS
POOL_BUF = max(POOL_WINDOWS) - 1
HEAD_DIM = 64
N_HEADS = D_MODEL // HEAD_DIM
W_B = N_HEADS * HEAD_DIM
Q_BLOCK = 128
EPS = 1e-6
NEG = -1e30
F_BIAS_CENTER = 3.0

kernel_name = 'yoco_pool_forgetting_attention_step'


def rmsnorm(x, g):
    x32 = x.astype(jnp.float32)
    y = x32 * lax.rsqrt(jnp.mean(x32 * x32, axis=-1, keepdims=True) + EPS)
    return (y * g.astype(jnp.float32)).astype(x.dtype)


def adaln(c, w, b):
    m = jnp.einsum('bd,de->be', jax.nn.silu(c), w) + b
    shift, scale, gate = jnp.split(m[:, None, :], 3, axis=-1)
    return shift, scale, gate


def pool_mixer(h, hist, start_pos, in_w, grp_w, scale, out_w):
    uz = h @ in_w
    u, z = jnp.split(uz, 2, axis=-1)
    full = jnp.concatenate([hist.astype(u.dtype), u], axis=1)
    p = hist.shape[1]
    n = u.shape[1]
    cs = jnp.cumsum(full.astype(jnp.float32), axis=1)
    cs = jnp.pad(cs, ((0, 0), (1, 0), (0, 0)))
    rows = p + jnp.arange(n)
    avail = start_pos + jnp.arange(n) + 1
    u32 = u.astype(jnp.float32)
    groups = []
    for g, w in enumerate(POOL_WINDOWS):
        sl = slice(g * GW, (g + 1) * GW)
        hi = cs[:, p + 1:, sl]
        lo = jnp.take(cs[:, :, sl], jnp.maximum(rows + 1 - w, 0), axis=1)
        cnt = jnp.minimum(avail, w).astype(jnp.float32)[None, :, None]
        groups.append((hi - lo) / cnt - u32[:, :, sl])
    pooled = jnp.stack(groups, axis=2).astype(u.dtype)
    mixed = jnp.einsum('bngc,gcd->bngd', pooled, grp_w).reshape(u.shape) * scale
    y = (mixed * jax.nn.silu(z)) @ out_w
    return y, full[:, -POOL_BUF:]


def shared_kv(x, kv_g, kv_w, f_b):
    b, n, _ = x.shape
    kvf = rmsnorm(x, kv_g) @ kv_w
    k = kvf[..., :W_B].reshape(b, n, N_HEADS, HEAD_DIM)
    v = kvf[..., W_B:2 * W_B].reshape(b, n, N_HEADS, HEAD_DIM)
    logf = jax.nn.log_sigmoid((kvf[..., 2 * W_B:] + f_b).astype(jnp.float32))
    return k, v, logf


def forget_attend(q, q_idx, k, v, fk, fq):
    s = jnp.einsum('bqhd,bkhd->bhqk', q, k, preferred_element_type=jnp.float32) * (HEAD_DIM ** -0.5)
    s = s + (jnp.transpose(fq, (0, 2, 1))[..., :, None] - jnp.transpose(fk, (0, 2, 1))[..., None, :])
    mask = jnp.arange(k.shape[1])[None, :] <= q_idx[:, None]
    s = jnp.where(mask, s, NEG)
    p = jax.nn.softmax(s, axis=-1)
    return jnp.einsum('bhqk,bkhd->bqhd', p.astype(v.dtype), v)


def prompt_attend(q, k, v, F):
    b, s = q.shape[:2]

    def one(i):
        start = i * Q_BLOCK
        qb = lax.dynamic_slice_in_dim(q, start, Q_BLOCK, axis=1)
        fqb = lax.dynamic_slice_in_dim(F, start, Q_BLOCK, axis=1)
        return forget_attend(qb, start + jnp.arange(Q_BLOCK), k, v, F, fqb)

    out = lax.map(one, jnp.arange(s // Q_BLOCK))
    return jnp.moveaxis(out, 0, 1).reshape(b, s, N_HEADS, HEAD_DIM)


def gather_pages(cache, page_table):
    g = cache[page_table]
    return g.reshape((g.shape[0], g.shape[1] * g.shape[2]) + g.shape[3:])


def trunk(x, c, pool_hist, start_pos, past, ada_w, ada_b, pre_g, post_g,
          a_in_w, a_grp_w, a_scale, a_out_w, kv_g, kv_w, f_b, b_in_w, b_out_w):
    new_pool = []
    k_new = v_new = logf_new = None
    for l in range(DEPTH):
        shift, scale, gate = adaln(c, ada_w[l], ada_b[l])
        h = rmsnorm(x, pre_g[l]) * (1 + scale) + shift
        if l < N_A:
            out, hist = pool_mixer(h, pool_hist[l], start_pos, a_in_w[l], a_grp_w[l], a_scale[l], a_out_w[l])
            new_pool.append(hist)
        else:
            if l == N_A:
                k_new, v_new, logf_new = shared_kv(x, kv_g, kv_w, f_b)
                if past is None:
                    k_all, v_all, lf_all = k_new, v_new, logf_new
                else:
                    k_all = jnp.concatenate([past[0].astype(k_new.dtype), k_new], axis=1)
                    v_all = jnp.concatenate([past[1].astype(v_new.dtype), v_new], axis=1)
                    lf_all = jnp.concatenate([past[2].astype(jnp.float32), logf_new], axis=1)
                F_all = jnp.cumsum(lf_all, axis=1)
                n_past = k_all.shape[1] - x.shape[1]
            j = l - N_A
            bsz, n, _ = h.shape
            qz = h @ b_in_w[j]
            q = qz[..., :W_B].reshape(bsz, n, N_HEADS, HEAD_DIM)
            zg = qz[..., W_B:]
            if past is None:
                o = prompt_attend(q, k_all, v_all, F_all)
            else:
                o = forget_attend(q, n_past + jnp.arange(n), k_all, v_all, F_all, F_all[:, n_past:])
            o = o.reshape(bsz, n, W_B).astype(h.dtype)
            out = (o * jax.nn.silu(zg)) @ b_out_w[j]
        x = x + gate * rmsnorm(out, post_g[l])
    return x, jnp.stack(new_pool, axis=0), k_new, v_new, logf_new


def setup_inputs(seed: int = 0) -> dict:
    key = jax.random.key(seed)
    ks = jax.random.split(key, 24)
    f32 = jnp.float32
    n_pages = PAST_LEN // PAGE_SIZE
    n_used = DEC_BATCH * n_pages
    n_phys = n_used + n_used // 4
    nrm = lambda k, s, sc: jax.random.normal(k, s, f32) * sc
    page_table = jax.random.permutation(ks[0], n_phys)[:n_used].reshape(DEC_BATCH, n_pages).astype(jnp.int32)
    kv_w = jnp.concatenate([
        nrm(ks[1], (D_MODEL, 2 * W_B), D_MODEL ** -0.5),
        nrm(ks[2], (D_MODEL, N_HEADS), 0.1 * D_MODEL ** -0.5)], axis=1)
    return {
        'x_prompt': nrm(ks[3], (BATCH, SEQ, D_MODEL), 1.0),
        'x_sample': nrm(ks[4], (DEC_BATCH, DEC_SEQ, D_MODEL), 1.0),
        'state_pool': nrm(ks[5], (N_A, DEC_BATCH, POOL_BUF, W_A), 1.0),
        'cache_k': nrm(ks[6], (n_phys, PAGE_SIZE, N_HEADS, HEAD_DIM), 1.0),
        'cache_v': nrm(ks[7], (n_phys, PAGE_SIZE, N_HEADS, HEAD_DIM), 1.0),
        'cache_logf': jax.nn.log_sigmoid(F_BIAS_CENTER + nrm(ks[8], (n_phys, PAGE_SIZE, N_HEADS), 0.5)),
        'page_table': page_table,
        'c_prompt': nrm(ks[9], (BATCH, D_MODEL), 1.0),
        'c_sample': nrm(ks[10], (DEC_BATCH, D_MODEL), 1.0),
        'ada_w': nrm(ks[11], (DEPTH, D_MODEL, 3 * D_MODEL), 0.5 * D_MODEL ** -0.5),
        'ada_b': nrm(ks[12], (DEPTH, 3 * D_MODEL), 0.02),
        'pre_g': 1.0 + nrm(ks[13], (DEPTH, D_MODEL), 0.05),
        'post_g': 1.0 + nrm(ks[14], (DEPTH, D_MODEL), 0.05),
        'a_in_w': nrm(ks[15], (N_A, D_MODEL, 2 * W_A), D_MODEL ** -0.5),
        'a_grp_w': nrm(ks[16], (N_A, N_POOL_GROUPS, GW, GW), GW ** -0.5),
        'a_scale': 1.0 + nrm(ks[17], (N_A, W_A), 0.1),
        'a_out_w': nrm(ks[18], (N_A, W_A, D_MODEL), W_A ** -0.5),
        'kv_g': 1.0 + nrm(ks[19], (D_MODEL,), 0.05),
        'kv_w': kv_w,
        'f_b': F_BIAS_CENTER + nrm(ks[20], (N_HEADS,), 0.5),
        'b_in_w': nrm(ks[21], (N_B, D_MODEL, 2 * W_B), D_MODEL ** -0.5),
        'b_out_w': nrm(ks[22], (N_B, W_B, D_MODEL), W_B ** -0.5),
    }


def reference(x_prompt, x_sample, state_pool, cache_k, cache_v, cache_logf, page_table,
              c_prompt, c_sample, ada_w, ada_b, pre_g, post_g, a_in_w, a_grp_w, a_scale,
              a_out_w, kv_g, kv_w, f_b, b_in_w, b_out_w):
    weights = (ada_w, ada_b, pre_g, post_g, a_in_w, a_grp_w, a_scale, a_out_w, kv_g, kv_w, f_b, b_in_w, b_out_w)
    empty_hist = [jnp.zeros((x_prompt.shape[0], 0, W_A), x_prompt.dtype)] * N_A
    y_prompt, pool_prompt, k_prompt, v_prompt, logf_prompt = trunk(
        x_prompt, c_prompt, empty_hist, 0, None, *weights)
    n_past = page_table.shape[1] * cache_k.shape[1]
    past = (gather_pages(cache_k, page_table), gather_pages(cache_v, page_table),
            gather_pages(cache_logf, page_table))
    y_sample, pool_sample, k_sample, v_sample, logf_sample = trunk(
        x_sample, c_sample, state_pool, n_past, past, *weights)
    return (y_prompt, y_sample, pool_prompt, pool_sample, k_prompt, v_prompt, logf_prompt,
            k_sample, v_sample, logf_sample)
```

```python
import functools

import jax
import jax.numpy as jnp
import numpy as np
from jax import lax
from jax.experimental import pallas as pl
from jax.experimental.pallas import tpu as pltpu

F32 = jnp.float32
BF16 = jnp.bfloat16

EPS = 1e-6
NEG = -1e30
POOL_WINDOWS = (2, 4, 8, 16)
POOL_BUF = max(POOL_WINDOWS) - 1
HEAD_DIM = 64
HALO = 16

V7X_VMEM_BYTES = 64 * 1024 * 1024
VMEM_LIMIT = V7X_VMEM_BYTES - 8 * 1024 * 1024
LANES = 128
BAND = 256


def _params(*sem):
    return pltpu.CompilerParams(dimension_semantics=sem, vmem_limit_bytes=VMEM_LIMIT)


def _const_spec(shape):
    nd = len(shape)
    return pl.BlockSpec(shape, lambda *_: (0,) * nd, pipeline_mode=pl.Buffered(1))


def _silu(x):
    return x * jax.nn.sigmoid(x)


def _rms(x):
    return x * lax.rsqrt(jnp.mean(x * x, axis=-1, keepdims=True) + EPS)


def _split3(x):
    a = x.astype(BF16)
    r = x - a.astype(F32)
    b = r.astype(BF16)
    c = (r - b.astype(F32)).astype(BF16)
    return a, b, c


def _log_sigmoid(x):
    return jnp.minimum(x, 0.0) - jnp.log1p(jnp.exp(-jnp.abs(x)))


def _div(x, n):
    return x >> (n.bit_length() - 1) if n & (n - 1) == 0 else x // n


def _mod(x, n):
    return x & (n - 1) if n & (n - 1) == 0 else x % n


def _dot(a, b):
    return jnp.dot(a, b, preferred_element_type=F32)


def _dot_nt(a, b):
    return lax.dot_general(a, b, (((1,), (1,)), ((), ())), preferred_element_type=F32)


def _dot_exact_rhs(sel, x):
    a, b, c = _split3(x)
    return _dot(sel, a) + _dot(sel, b) + _dot(sel, c)


def _dot_exact_lhs(x, sel):
    a, b, c = _split3(x)
    return _dot(a, sel) + _dot(b, sel) + _dot(c, sel)


def _adaln_kernel(c_ref, w_ref, b_ref, o_ref):
    a = _silu(c_ref[...]).astype(BF16)
    o_ref[0] = _dot(a, w_ref[0].astype(BF16)) + b_ref[0]


def _adaln(c_all, ada_w, ada_b):
    depth, d, d3 = ada_w.shape
    rows = c_all.shape[0]
    tn = d3 // 2
    return pl.pallas_call(
        _adaln_kernel,
        grid=(depth, d3 // tn),
        in_specs=[
            pl.BlockSpec((rows, d), lambda l, j: (0, 0)),
            pl.BlockSpec((1, d, tn), lambda l, j: (l, 0, j)),
            pl.BlockSpec((1, 1, tn), lambda l, j: (l, 0, j)),
        ],
        out_specs=pl.BlockSpec((1, rows, tn), lambda l, j: (l, 0, j)),
        out_shape=jax.ShapeDtypeStruct((depth, rows, d3), F32),
        compiler_params=_params("arbitrary", "arbitrary"),
        name="adaln",
    )(c_all, ada_w, ada_b.reshape(depth, 1, d3))


def _band_constants():
    t = np.arange(BAND)[:, None]
    s = np.arange(BAND)[None, :]
    eye = (t == s).astype(np.float32)

    def one(w, first):
        cnt = np.minimum(t + 1, w) if first else w
        return ((t - s >= 1) & (t - s < w)).astype(np.float32) - (cnt - 1) * eye

    band = np.stack([np.stack([one(w, first) for w in POOL_WINDOWS]) for first in (True, False)])
    th = np.arange(HALO)[:, None]
    jh = np.arange(HALO)[None, :]
    bandh = np.stack([(th - jh + HALO < w) for w in POOL_WINDOWS]).astype(np.float32)
    return jnp.asarray(band, BF16), jnp.asarray(bandh, BF16)


def _pool_prompt_kernel(x_ref, shift_ref, scale_ref, gate_ref, preg_ref, postg_ref, inw_ref,
                        band_ref, bandh_ref, grpw_ref, ascale_ref, outw_ref,
                        x1_ref, tail_ref, halo_ref, *, tile, w_a):
    t = pl.program_id(1)
    gw = w_a // len(POOL_WINDOWS)

    @pl.when(t == 0)
    def _():
        halo_ref[...] = jnp.zeros_like(halo_ref)

    x = x_ref[0]
    h = (_rms(x) * preg_ref[...]) * (1.0 + scale_ref[0]) + shift_ref[0]
    uz = _dot(h.astype(BF16), inw_ref[...])
    u = uz[:, :w_a]
    z = uz[:, w_a:]
    u_bf = u.astype(BF16)

    gated_blocks = []
    for blk in range(tile // BAND):
        r0 = blk * BAND
        halo = halo_ref[...] if blk == 0 else u_bf[r0 - HALO:r0]
        pos = t * tile + r0 + lax.broadcasted_iota(jnp.int32, (BAND, 1), 0)
        kind = jnp.where(t == 0, 0, 1) if blk == 0 else 1
        cols = []
        for g, w in enumerate(POOL_WINDOWS):
            c0 = g * gw
            ug = u_bf[r0:r0 + BAND, c0:c0 + gw]
            wsum = _dot(band_ref[kind, g], ug)
            top = wsum[:HALO] + _dot(bandh_ref[g], halo[:, c0:c0 + gw])
            wsum = jnp.concatenate([top, wsum[HALO:]], axis=0)
            cnt = jnp.minimum(pos + 1, w).astype(F32)
            pooled = wsum / cnt
            mixed = _dot(pooled.astype(BF16), grpw_ref[g]) * ascale_ref[:, c0:c0 + gw]
            cols.append((mixed * _silu(z[r0:r0 + BAND, c0:c0 + gw])).astype(BF16))
        gated_blocks.append(jnp.concatenate(cols, axis=1))
    gated = gated_blocks[0] if len(gated_blocks) == 1 else jnp.concatenate(gated_blocks, axis=0)

    y = _dot(gated, outw_ref[...])
    x1_ref[0] = x + gate_ref[0] * (_rms(y) * postg_ref[...])

    halo_ref[...] = u_bf[tile - HALO:]

    @pl.when(t == pl.num_programs(1) - 1)
    def _():
        tail_ref[0] = u[tile - HALO:]


def _pool_prompt(x, shift, scale, gate, pre_g, post_g, in_w, grp_w, a_scale, out_w, *, tile=256):
    b, s, d = x.shape
    w_a = out_w.shape[0]
    band, bandh = _band_constants()
    tok = pl.BlockSpec((1, tile, d), lambda i, j: (i, j, 0))
    mod = pl.BlockSpec((1, 1, d), lambda i, j: (i, 0, 0))
    return pl.pallas_call(
        functools.partial(_pool_prompt_kernel, tile=tile, w_a=w_a),
        grid=(b, s // tile),
        in_specs=[tok, mod, mod, mod, _const_spec((1, d)), _const_spec((1, d)),
                  _const_spec(in_w.shape), _const_spec(band.shape), _const_spec(bandh.shape),
                  _const_spec(grp_w.shape), _const_spec((1, w_a)), _const_spec(out_w.shape)],
        out_specs=[tok, pl.BlockSpec((1, HALO, w_a), lambda i, j: (i, 0, 0))],
        out_shape=[jax.ShapeDtypeStruct((b, s, d), F32),
                   jax.ShapeDtypeStruct((b, HALO, w_a), F32)],
        scratch_shapes=[pltpu.VMEM((HALO, w_a), BF16)],
        compiler_params=_params("arbitrary", "arbitrary"),
        name="pool_prompt",
    )(x, shift, scale, gate, pre_g, post_g, in_w, band, bandh, grp_w, a_scale, out_w)


def _pool_sample_kernel(x_ref, shift_ref, scale_ref, gate_ref, preg_ref, postg_ref, inw_ref,
                        hist_ref, grpw_ref, ascale_ref, outw_ref, x1_ref, u_ref, *, steps, w_a):
    gw = w_a // len(POOL_WINDOWS)
    rows = x_ref.shape[0]
    nh = hist_ref.shape[0]

    x = x_ref[...]
    h = (_rms(x) * preg_ref[...]) * (1.0 + scale_ref[...]) + shift_ref[...]
    uz = _dot(h.astype(BF16), inw_ref[...])
    u = uz[:, :w_a]
    z = uz[:, w_a:]
    u_ref[...] = u
    u_bf = u.astype(BF16)
    hist_bf = hist_ref[...].astype(BF16)

    n_batch = nh // POOL_BUF
    ro = lax.broadcasted_iota(jnp.int32, (rows, rows), 0)
    ri = lax.broadcasted_iota(jnp.int32, (rows, rows), 1)
    same = _div(ro, steps) == _div(ri, steps)
    lag_new = _mod(ro, steps) - _mod(ri, steps)
    roh = lax.broadcasted_iota(jnp.int32, (rows, nh), 0)
    ch = lax.broadcasted_iota(jnp.int32, (rows, nh), 1)
    in_batch = _mod(ch, n_batch) == _div(roh, steps)
    lag_hist = POOL_BUF + _mod(roh, steps) - _div(ch, n_batch)

    cols = []
    for g, w in enumerate(POOL_WINDOWS):
        c0 = g * gw
        sel_new = jnp.where(same & (lag_new >= 1) & (lag_new < w), 1.0, 0.0)
        sel_new = jnp.where(ro == ri, -(w - 1.0), sel_new).astype(BF16)
        sel_hist = jnp.where(in_batch & (lag_hist < w), 1.0, 0.0).astype(BF16)
        wsum = _dot(sel_new, u_bf[:, c0:c0 + gw]) + _dot(sel_hist, hist_bf[:, c0:c0 + gw])
        pooled = wsum / float(w)
        mixed = _dot(pooled.astype(BF16), grpw_ref[g]) * ascale_ref[:, c0:c0 + gw]
        cols.append((mixed * _silu(z[:, c0:c0 + gw])).astype(BF16))
    gated = jnp.concatenate(cols, axis=1)
    y = _dot(gated, outw_ref[...])
    x1_ref[...] = x + gate_ref[...] * (_rms(y) * postg_ref[...])


def _pool_sample(x, shift, scale, gate, pre_g, post_g, in_w, hist, grp_w, a_scale, out_w, *, steps):
    rows, d = x.shape
    w_a = out_w.shape[0]
    args = (x, shift, scale, gate, pre_g, post_g, in_w, hist, grp_w, a_scale, out_w)
    return pl.pallas_call(
        functools.partial(_pool_sample_kernel, steps=steps, w_a=w_a),
        grid=(1,),
        in_specs=[_const_spec(a.shape) for a in args],
        out_specs=[pl.BlockSpec((rows, d), lambda i: (0, 0)),
                   pl.BlockSpec((rows, w_a), lambda i: (0, 0))],
        out_shape=[jax.ShapeDtypeStruct((rows, d), F32),
                   jax.ShapeDtypeStruct((rows, w_a), F32)],
        compiler_params=_params("arbitrary"),
        name="pool_sample",
    )(*args)


def _proj_kernel(x_ref, shift_ref, scale_ref, kvg_ref, preg_ref, kvw_ref, wf_ref, fb_ref, binw_ref,
                 *rest, tile, d, prompt):
    if prompt:
        utri_ref, k_ref, v_ref, sz_ref, kb_ref, vb_ref, qb_ref, lft_ref, ft_ref, carry_ref = rest
    else:
        k_ref, v_ref, sz_ref, logf_ref, q_ref = rest
    n_heads = d // HEAD_DIM

    r = _rms(x_ref[0])
    xn = (r * kvg_ref[...]).astype(BF16)
    h = ((r * preg_ref[...]) * (1.0 + scale_ref[0]) + shift_ref[0]).astype(BF16)

    kv = _dot(xn, kvw_ref[...])
    k = kv[:, :d]
    v = kv[:, d:]
    k_ref[0] = k
    v_ref[0] = v
    logf = _log_sigmoid(_dot(xn, wf_ref[...]) + fb_ref[...])

    qz = _dot(h, binw_ref[...])
    q = qz[:, :d] * (HEAD_DIM ** -0.5)
    sz_ref[0] = _silu(qz[:, d:]).astype(BF16)

    if prompt:
        for hp in range(d // LANES):
            sl = slice(hp * LANES, (hp + 1) * LANES)
            kb_ref[0, hp] = k[:, sl].astype(BF16)
            vb_ref[0, hp] = v[:, sl].astype(BF16)
            qb_ref[0, hp] = q[:, sl].astype(BF16)

        @pl.when(pl.program_id(1) == 0)
        def _():
            carry_ref[...] = jnp.zeros_like(carry_ref)

        lft = logf.T[:n_heads]
        lft_ref[0] = lft
        cum = _dot_exact_lhs(lft, utri_ref[...]) + carry_ref[...]
        ft_ref[0] = cum
        carry_ref[...] = cum[:, tile - 1:tile]
    else:
        logf_ref[0] = logf[:, :n_heads]
        q_ref[0] = q


def _proj(x, shift, scale, kv_g, pre_g, kv_w_main, w_f, f_b, b_in_w, *, tile, prompt):
    b, s, d = x.shape
    n_heads = d // HEAD_DIM
    hp = d // LANES
    tok = pl.BlockSpec((1, tile, d), lambda i, j: (i, j, 0))
    mod_rows = shift.shape[1]
    mod_tile = tile if mod_rows == s else 1
    mod = pl.BlockSpec((1, mod_tile, d), (lambda i, j: (i, j, 0)) if mod_rows == s else (lambda i, j: (i, 0, 0)))
    in_specs = [tok, mod, mod, _const_spec((1, d)), _const_spec((1, d)), _const_spec(kv_w_main.shape),
                _const_spec(w_f.shape), _const_spec(f_b.shape), _const_spec(b_in_w.shape)]
    args = [x, shift, scale, kv_g, pre_g, kv_w_main, w_f, f_b, b_in_w]
    out_specs = [tok, tok, tok]
    out_shape = [jax.ShapeDtypeStruct((b, s, d), F32), jax.ShapeDtypeStruct((b, s, d), F32),
                 jax.ShapeDtypeStruct((b, s, d), BF16)]
    scratch = []
    if prompt:
        utri = jnp.asarray(np.triu(np.ones((tile, tile), np.float32)), BF16)
        in_specs.append(_const_spec(utri.shape))
        args.append(utri)
        pair = pl.BlockSpec((1, hp, tile, LANES), lambda i, j: (i, 0, j, 0))
        head_rows = pl.BlockSpec((1, n_heads, tile), lambda i, j: (i, 0, j))
        out_specs += [pair, pair, pair, head_rows, head_rows]
        out_shape += [jax.ShapeDtypeStruct((b, hp, s, LANES), BF16)] * 3
        out_shape += [jax.ShapeDtypeStruct((b, n_heads, s), F32)] * 2
        scratch = [pltpu.VMEM((n_heads, 1), F32)]
    else:
        out_specs += [pl.BlockSpec((1, tile, n_heads), lambda i, j: (i, j, 0)), tok]
        out_shape += [jax.ShapeDtypeStruct((b, s, n_heads), F32), jax.ShapeDtypeStruct((b, s, d), F32)]
    return pl.pallas_call(
        functools.partial(_proj_kernel, tile=tile, d=d, prompt=prompt),
        grid=(b, s // tile),
        in_specs=in_specs,
        out_specs=out_specs,
        out_shape=out_shape,
        scratch_shapes=scratch,
        compiler_params=_params("arbitrary", "arbitrary"),
        name="proj_prompt" if prompt else "proj_sample",
    )(*args)


def _attn_kernel(q_ref, k_ref, v_ref, f_ref, o_ref, *, tq):
    qi = pl.program_id(2)
    q = q_ref[0, 0]
    lane = lax.broadcasted_iota(jnp.int32, (1, LANES), 1)
    first = lane < HEAD_DIM
    zero = jnp.zeros_like(q)
    qh = (jnp.where(first, q, zero), jnp.where(first, zero, q))

    def step(j, carry, diagonal):
        k0 = pl.multiple_of(j * tq, tq)
        kt = k_ref[0, 0, pl.ds(k0, tq), :]
        vt = v_ref[0, 0, pl.ds(k0, tq), :]
        out = []
        for hd in range(2):
            m, l, acc = carry[hd]
            s = _dot_nt(qh[hd], kt) - f_ref[0, 0, hd:hd + 1, pl.ds(k0, tq)]
            if diagonal:
                row = lax.broadcasted_iota(jnp.int32, (tq, tq), 0)
                col = lax.broadcasted_iota(jnp.int32, (tq, tq), 1)
                s = jnp.where(col <= row, s, NEG)
            m_new = jnp.maximum(m, jnp.max(s, axis=1, keepdims=True))
            alpha = jnp.exp(m - m_new)
            p = jnp.exp(s - m_new)
            l = alpha * l + jnp.sum(p, axis=1, keepdims=True)
            acc = alpha * acc + _dot(p.astype(BF16), vt)
            out.append((m_new, l, acc))
        return tuple(out)

    init = tuple((jnp.full((tq, 1), NEG, F32), jnp.zeros((tq, 1), F32), jnp.zeros((tq, LANES), F32))
                 for _ in range(2))
    carry = lax.fori_loop(0, qi, functools.partial(step, diagonal=False), init)
    (_, l0, a0), (_, l1, a1) = step(qi, carry, True)
    o_ref[0] = jnp.where(first, a0 / l0, a1 / l1)


def _attn_prompt(qb, kb, vb, ft, *, tq=256):
    b, hp, s, _ = qb.shape
    ft = ft.reshape(b, hp, 2, s)
    full = pl.BlockSpec((1, 1, s, LANES), lambda i, p, j: (i, p, 0, 0))
    return pl.pallas_call(
        functools.partial(_attn_kernel, tq=tq),
        grid=(b, hp, s // tq),
        in_specs=[pl.BlockSpec((1, 1, tq, LANES), lambda i, p, j: (i, p, j, 0)), full, full,
                  pl.BlockSpec((1, 1, 2, s), lambda i, p, j: (i, p, 0, 0))],
        out_specs=pl.BlockSpec((1, tq, LANES), lambda i, p, j: (i, j, p)),
        out_shape=jax.ShapeDtypeStruct((b, s, hp * LANES), F32),
        compiler_params=_params("arbitrary", "arbitrary", "arbitrary"),
        name="attn_prompt",
    )(qb, kb, vb, ft)


def _sample_attn_kernel(pt_ref, q_ref, knew_ref, vnew_ref, lfnew_ref, *rest, pages, steps, d):
    k_pages = rest[:pages]
    v_pages = rest[pages:2 * pages]
    f_pages = rest[2 * pages:3 * pages]
    o_ref, qbd_ref, acc_ref, m_ref, l_ref, fc_ref, knew_buf, vnew_buf, lfnew_buf = rest[3 * pages:]
    del pt_ref
    c = pl.program_id(1)
    n_heads = d // HEAD_DIM
    page = k_pages[0].shape[2]
    cols = n_heads * steps

    col_r = lax.broadcasted_iota(jnp.int32, (LANES, 1), 0)
    col_l = lax.broadcasted_iota(jnp.int32, (1, LANES), 1)

    def to_lanes(col):
        return jnp.broadcast_to(col, (LANES, LANES)).T[0:1, :]

    @pl.when((pl.program_id(0) == 0) & (c == 0))
    def _():
        knew_buf[...] = jnp.zeros_like(knew_buf)
        vnew_buf[...] = jnp.zeros_like(vnew_buf)
        lfnew_buf[...] = jnp.zeros_like(lfnew_buf)

    @pl.when(c == 0)
    def _():
        q = q_ref[0]
        lane_head = _div(lax.broadcasted_iota(jnp.int32, (LANES, d), 1), HEAD_DIM)
        row = lax.broadcasted_iota(jnp.int32, (LANES, d), 0)
        live = (row < cols) & (lane_head == _div(row, steps))
        qbd = jnp.zeros((LANES, d), F32)
        for i in range(steps):
            qbd = jnp.where(live & (_mod(row, steps) == i), q[i:i + 1, :], qbd)
        qbd_ref[...] = qbd.astype(BF16)
        m_ref[...] = jnp.full_like(m_ref, NEG)
        l_ref[...] = jnp.zeros_like(l_ref)
        acc_ref[...] = jnp.zeros_like(acc_ref)
        fc_ref[...] = jnp.zeros_like(fc_ref)

    expand = jnp.where((col_r < cols) & (lax.broadcasted_iota(jnp.int32, (LANES, n_heads), 1) == _div(col_r, steps)),
                       1.0, 0.0).astype(BF16)
    tri_r = lax.broadcasted_iota(jnp.int32, (BAND, BAND), 0)
    tri_c = lax.broadcasted_iota(jnp.int32, (BAND, BAND), 1)
    utri = jnp.where(tri_r <= tri_c, 1.0, 0.0).astype(BF16)

    def chunk(with_new):
        k_list = [r[0] for r in k_pages]
        v_list = [r[0] for r in v_pages]
        f_list = [r[0] for r in f_pages]
        if with_new:
            knew_buf[0:steps, :] = knew_ref[0]
            vnew_buf[0:steps, :] = vnew_ref[0]
            lfnew_buf[0:steps, 0:n_heads] = lfnew_ref[0]
            k_list.append(knew_buf[...].T)
            v_list.append(vnew_buf[...].T)
            f_list.append(lfnew_buf[...].T[:n_heads])
        kt = jnp.concatenate(k_list, axis=1).astype(BF16)
        vt = jnp.concatenate(v_list, axis=1).astype(BF16)
        lf = jnp.concatenate(f_list, axis=1)
        keys = kt.shape[1]

        st = _dot(qbd_ref[...], kt)
        lfe = _dot_exact_rhs(expand, lf)
        fc = fc_ref[...]
        cum = []
        for c0 in range(0, keys, BAND):
            n = min(BAND, keys - c0)
            blk = _dot_exact_lhs(lfe[:, c0:c0 + n], utri[:n, :n]) + fc
            fc = blk[:, n - 1:n]
            cum.append(blk)
        fc_ref[...] = fc
        st = st - jnp.concatenate(cum, axis=1)
        if with_new:
            key = lax.broadcasted_iota(jnp.int32, (1, keys), 1) - pages * page
            st = jnp.where((key < 0) | (key <= _mod(col_r, steps)), st, NEG)

        m = m_ref[...]
        m_new = jnp.maximum(m, jnp.max(st, axis=1, keepdims=True))
        alpha = jnp.exp(m - m_new)
        p = jnp.exp(st - m_new)
        l_ref[...] = alpha * l_ref[...] + jnp.sum(p, axis=1, keepdims=True)
        m_ref[...] = m_new
        acc_ref[...] = acc_ref[...] * to_lanes(alpha) + _dot(vt, p.T.astype(BF16))

    last = pl.num_programs(1) - 1

    @pl.when(c < last)
    def _():
        chunk(False)

    @pl.when(c == last)
    def _():
        chunk(True)
        res = acc_ref[...] / to_lanes(l_ref[...])
        row_head = _div(lax.broadcasted_iota(jnp.int32, (d, LANES), 0), HEAD_DIM)
        col = lax.broadcasted_iota(jnp.int32, (d, LANES), 1)
        res = jnp.where((col < cols) & (row_head == _div(col, steps)), res, 0.0)
        pick = jnp.where((col_l < cols) & (lax.broadcasted_iota(jnp.int32, (8, LANES), 0) == _mod(col_l, steps)),
                         1.0, 0.0).astype(BF16)
        a, b, c3 = _split3(res)
        o_ref[0] = (_dot_nt(pick, a) + _dot_nt(pick, b) + _dot_nt(pick, c3))[:steps]


def _attn_sample(page_table, q, k_new, v_new, lf_new, cache_k, cache_v, cache_logf, *, pages=8):
    nb, steps, d = q.shape
    n_pages = page_table.shape[1]
    n_phys, page, n_heads = cache_logf.shape
    cache_k = jnp.transpose(cache_k, (0, 2, 3, 1)).reshape(n_phys, d, page)
    cache_v = jnp.transpose(cache_v, (0, 2, 3, 1)).reshape(n_phys, d, page)
    cache_logf = jnp.transpose(cache_logf, (0, 2, 1))
    pt_flat = page_table.reshape(-1)

    def page_map(r):
        return lambda i, c, pt: (pt[i * n_pages + c * pages + r], 0, 0)

    row = lambda i, c, pt: (i, 0, 0)
    kv_specs = [pl.BlockSpec((1, d, page), page_map(r)) for r in range(pages)]
    f_specs = [pl.BlockSpec((1, n_heads, page), page_map(r)) for r in range(pages)]
    grid_spec = pltpu.PrefetchScalarGridSpec(
        num_scalar_prefetch=1,
        grid=(nb, n_pages // pages),
        in_specs=[pl.BlockSpec((1, steps, d), row), pl.BlockSpec((1, steps, d), row),
                  pl.BlockSpec((1, steps, d), row), pl.BlockSpec((1, steps, n_heads), row)]
                 + kv_specs + kv_specs + f_specs,
        out_specs=pl.BlockSpec((1, steps, d), row),
        scratch_shapes=[pltpu.VMEM((LANES, d), BF16), pltpu.VMEM((d, LANES), F32),
                        pltpu.VMEM((LANES, 1), F32), pltpu.VMEM((LANES, 1), F32),
                        pltpu.VMEM((LANES, 1), F32), pltpu.VMEM((LANES, d), F32),
                        pltpu.VMEM((LANES, d), F32), pltpu.VMEM((LANES, LANES), F32)],
    )
    return pl.pallas_call(
        functools.partial(_sample_attn_kernel, pages=pages, steps=steps, d=d),
        grid_spec=grid_spec,
        out_shape=jax.ShapeDtypeStruct((nb, steps, d), F32),
        compiler_params=_params("arbitrary", "arbitrary"),
        name="attn_sample",
    )(pt_flat, q, k_new, v_new, lf_new, *([cache_k] * pages), *([cache_v] * pages), *([cache_logf] * pages))


def _outproj_kernel(o_ref, sz_ref, w_ref, postg_ref, gate_ref, x_ref, y_ref):
    g = (o_ref[0] * sz_ref[0].astype(F32)).astype(BF16)
    out = _dot(g, w_ref[...])
    y_ref[0] = x_ref[0] + gate_ref[0] * (_rms(out) * postg_ref[...])


def _outproj(o, sz, w, post_g, gate, x, *, tile, name):
    b, s, d = x.shape
    tok = pl.BlockSpec((1, tile, d), lambda i, j: (i, j, 0))
    per_row = gate.shape[1] == s
    mod = pl.BlockSpec((1, tile if per_row else 1, d),
                       (lambda i, j: (i, j, 0)) if per_row else (lambda i, j: (i, 0, 0)))
    return pl.pallas_call(
        _outproj_kernel,
        grid=(b, s // tile),
        in_specs=[tok, tok, _const_spec(w.shape), _const_spec((1, d)), mod, tok],
        out_specs=tok,
        out_shape=jax.ShapeDtypeStruct((b, s, d), F32),
        compiler_params=_params("arbitrary", "arbitrary"),
        name=name,
    )(o, sz, w, post_g, gate, x)


def kernel(x_prompt, x_sample, state_pool, cache_k, cache_v, cache_logf, page_table, c_prompt, c_sample,
           ada_w, ada_b, pre_g, post_g, a_in_w, a_grp_w, a_scale, a_out_w, kv_g, kv_w, f_b, b_in_w, b_out_w):
    bp, seq, d = x_prompt.shape
    bs, steps, _ = x_sample.shape
    n_heads = d // HEAD_DIM
    w_a = a_out_w.shape[1]
    rows_s = bs * steps

    pad = (-(bp + bs)) % 8
    c_all = jnp.concatenate([c_prompt, c_sample, jnp.zeros((pad, d), F32)], axis=0)
    mod = _adaln(c_all, ada_w, ada_b)

    def mods(layer):
        m = mod[layer]
        parts = [m[:, i * d:(i + 1) * d] for i in range(3)]
        prompt = [p[:bp].reshape(bp, 1, d) for p in parts]
        sample = [jnp.repeat(p[bp:bp + bs], steps, axis=0) for p in parts]
        return prompt, sample

    (shift0_p, scale0_p, gate0_p), (shift0_s, scale0_s, gate0_s) = mods(0)
    (shift1_p, scale1_p, gate1_p), (shift1_s, scale1_s, gate1_s) = mods(1)

    in_w = a_in_w[0].astype(BF16)
    grp_w = a_grp_w[0].astype(BF16)
    out_w = a_out_w[0].astype(BF16)
    kv_w_main = kv_w[:, :2 * d].astype(BF16)
    w_f = jnp.pad(kv_w[:, 2 * d:], ((0, 0), (0, LANES - n_heads))).astype(BF16)
    f_b_pad = jnp.pad(f_b, (0, LANES - n_heads)).reshape(1, LANES)
    bin_w = b_in_w[0].astype(BF16)
    bout_w = b_out_w[0].astype(BF16)
    pre0, pre1 = pre_g[0].reshape(1, d), pre_g[1].reshape(1, d)
    post0, post1 = post_g[0].reshape(1, d), post_g[1].reshape(1, d)
    kvg = kv_g.reshape(1, d)
    asc = a_scale[0].reshape(1, w_a)

    x1_p, tail_p = _pool_prompt(x_prompt, shift0_p, scale0_p, gate0_p, pre0, post0, in_w, grp_w, asc, out_w)
    pool_prompt = tail_p[None, :, HALO - POOL_BUF:, :]
    k_p, v_p, sz_p, kb, vb, qb, lft_p, ft = _proj(
        x1_p, shift1_p, scale1_p, kvg, pre1, kv_w_main, w_f, f_b_pad, bin_w, tile=512, prompt=True)
    logf_p = jnp.transpose(lft_p, (0, 2, 1))
    o_p = _attn_prompt(qb, kb, vb, ft)
    y_prompt = _outproj(o_p, sz_p, bout_w, post1, gate1_p, x1_p, tile=512, name="outproj_prompt")

    hist = jnp.transpose(state_pool[0], (1, 0, 2)).reshape(POOL_BUF * bs, w_a)
    x1_s, u_s = _pool_sample(x_sample.reshape(rows_s, d), shift0_s, scale0_s, gate0_s, pre0, post0,
                             in_w, hist, grp_w, asc, out_w, steps=steps)
    pool_sample = jnp.concatenate([state_pool[:, :, steps:], u_s.reshape(1, bs, steps, w_a)], axis=2)
    x1_s3 = x1_s.reshape(1, rows_s, d)
    k_s, v_s, sz_s, logf_s, q_s = _proj(
        x1_s3, shift1_s[None], scale1_s[None], kvg, pre1, kv_w_main, w_f, f_b_pad, bin_w,
        tile=rows_s, prompt=False)
    o_s = _attn_sample(page_table, q_s.reshape(bs, steps, d), k_s.reshape(bs, steps, d),
                       v_s.reshape(bs, steps, d), logf_s.reshape(bs, steps, n_heads),
                       cache_k, cache_v, cache_logf)
    y_sample = _outproj(o_s.reshape(1, rows_s, d), sz_s, bout_w, post1, gate1_s[None], x1_s3,
                        tile=rows_s, name="outproj_sample")

    return (y_prompt, y_sample.reshape(bs, steps, d), pool_prompt, pool_sample,
            k_p.reshape(bp, seq, n_heads, HEAD_DIM), v_p.reshape(bp, seq, n_heads, HEAD_DIM), logf_p,
            k_s.reshape(bs, steps, n_heads, HEAD_DIM), v_s.reshape(bs, steps, n_heads, HEAD_DIM),
            logf_s.reshape(bs, steps, n_heads))
```

```python
import functools

import jax
import jax.numpy as jnp
import numpy as np
from jax import lax
from jax.experimental import pallas as pl
from jax.experimental.pallas import tpu as pltpu

F32 = jnp.float32
BF16 = jnp.bfloat16

EPS = 1e-6
NEG = -1e30
LOG2E = 1.4426950408889634
POOL_WINDOWS = (2, 4, 8, 16)
POOL_BUF = max(POOL_WINDOWS) - 1
HEAD_DIM = 64
HALO = 16

V7X_VMEM_BYTES = 64 * 1024 * 1024
VMEM_LIMIT = V7X_VMEM_BYTES - 8 * 1024 * 1024
LANES = 128
BAND = 256


def _params(*sem):
    return pltpu.CompilerParams(dimension_semantics=sem, vmem_limit_bytes=VMEM_LIMIT)


def _const_spec(shape):
    nd = len(shape)
    return pl.BlockSpec(shape, lambda *_: (0,) * nd, pipeline_mode=pl.Buffered(1))


def _silu(x):
    return x * jax.nn.sigmoid(x)


def _rms(x):
    return x * lax.rsqrt(jnp.mean(x * x, axis=-1, keepdims=True) + EPS)


def _split3(x):
    a = x.astype(BF16)
    r = x - a.astype(F32)
    b = r.astype(BF16)
    c = (r - b.astype(F32)).astype(BF16)
    return a, b, c


def _log_sigmoid(x):
    return jnp.minimum(x, 0.0) - jnp.log1p(jnp.exp(-jnp.abs(x)))


def _div(x, n):
    return x >> (n.bit_length() - 1) if n & (n - 1) == 0 else x // n


def _mod(x, n):
    return x & (n - 1) if n & (n - 1) == 0 else x % n


def _dot(a, b):
    return jnp.dot(a, b, preferred_element_type=F32)


def _dot_nt(a, b):
    return lax.dot_general(a, b, (((1,), (1,)), ((), ())), preferred_element_type=F32)


def _dot_exact_rhs(sel, x):
    a, b, c = _split3(x)
    return _dot(sel, a) + _dot(sel, b) + _dot(sel, c)


def _dot_exact_lhs(x, sel):
    a, b, c = _split3(x)
    return _dot(a, sel) + _dot(b, sel) + _dot(c, sel)


def _adaln_kernel(c_ref, w_ref, b_ref, o_ref):
    a = _silu(c_ref[...]).astype(BF16)
    o_ref[0] = _dot(a, w_ref[0].astype(BF16)) + b_ref[0]


def _adaln(c_all, ada_w, ada_b):
    depth, d, d3 = ada_w.shape
    rows = c_all.shape[0]
    tn = d3 // 2
    return pl.pallas_call(
        _adaln_kernel,
        grid=(depth, d3 // tn),
        in_specs=[
            pl.BlockSpec((rows, d), lambda l, j: (0, 0)),
            pl.BlockSpec((1, d, tn), lambda l, j: (l, 0, j)),
            pl.BlockSpec((1, 1, tn), lambda l, j: (l, 0, j)),
        ],
        out_specs=pl.BlockSpec((1, rows, tn), lambda l, j: (l, 0, j)),
        out_shape=jax.ShapeDtypeStruct((depth, rows, d3), F32),
        compiler_params=_params("arbitrary", "arbitrary"),
        name="adaln",
    )(c_all, ada_w, ada_b.reshape(depth, 1, d3))


def _band_constants():
    t = np.arange(BAND)[:, None]
    s = np.arange(BAND)[None, :]
    eye = (t == s).astype(np.float32)

    def one(w, first):
        cnt = np.minimum(t + 1, w) if first else w
        return ((t - s >= 1) & (t - s < w)).astype(np.float32) - (cnt - 1) * eye

    band = np.stack([np.stack([one(w, first) for w in POOL_WINDOWS]) for first in (True, False)])
    th = np.arange(HALO)[:, None]
    jh = np.arange(HALO)[None, :]
    bandh = np.stack([(th - jh + HALO < w) for w in POOL_WINDOWS]).astype(np.float32)
    return jnp.asarray(band, BF16), jnp.asarray(bandh, BF16)


def _pool_prompt_kernel(x_ref, shift_ref, scale_ref, gate_ref, preg_ref, postg_ref, inw_ref,
                        band_ref, bandh_ref, grpw_ref, ascale_ref, outw_ref,
                        x1_ref, tail_ref, halo_ref, *, tile, w_a):
    t = pl.program_id(1)
    gw = w_a // len(POOL_WINDOWS)

    @pl.when(t == 0)
    def _():
        halo_ref[...] = jnp.zeros_like(halo_ref)

    x = x_ref[0]
    h = (_rms(x) * preg_ref[...]) * (1.0 + scale_ref[0]) + shift_ref[0]
    uz = _dot(h.astype(BF16), inw_ref[...])
    u = uz[:, :w_a]
    z = uz[:, w_a:]
    u_bf = u.astype(BF16)

    gated_blocks = []
    for blk in range(tile // BAND):
        r0 = blk * BAND
        halo = halo_ref[...] if blk == 0 else u_bf[r0 - HALO:r0]
        pos = t * tile + r0 + lax.broadcasted_iota(jnp.int32, (BAND, 1), 0)
        kind = jnp.where(t == 0, 0, 1) if blk == 0 else 1
        cols = []
        for g, w in enumerate(POOL_WINDOWS):
            c0 = g * gw
            ug = u_bf[r0:r0 + BAND, c0:c0 + gw]
            wsum = _dot(band_ref[kind, g], ug)
            top = wsum[:HALO] + _dot(bandh_ref[g], halo[:, c0:c0 + gw])
            wsum = jnp.concatenate([top, wsum[HALO:]], axis=0)
            cnt = jnp.minimum(pos + 1, w).astype(F32)
            pooled = wsum / cnt
            mixed = _dot(pooled.astype(BF16), grpw_ref[g]) * ascale_ref[:, c0:c0 + gw]
            cols.append((mixed * _silu(z[r0:r0 + BAND, c0:c0 + gw])).astype(BF16))
        gated_blocks.append(jnp.concatenate(cols, axis=1))
    gated = gated_blocks[0] if len(gated_blocks) == 1 else jnp.concatenate(gated_blocks, axis=0)

    y = _dot(gated, outw_ref[...])
    x1_ref[0] = x + gate_ref[0] * (_rms(y) * postg_ref[...])

    halo_ref[...] = u_bf[tile - HALO:]

    @pl.when(t == pl.num_programs(1) - 1)
    def _():
        tail_ref[0] = u[tile - HALO:]


def _pool_prompt(x, shift, scale, gate, pre_g, post_g, in_w, grp_w, a_scale, out_w, *, tile=256):
    b, s, d = x.shape
    w_a = out_w.shape[0]
    band, bandh = _band_constants()
    tok = pl.BlockSpec((1, tile, d), lambda i, j: (i, j, 0))
    mod = pl.BlockSpec((1, 1, d), lambda i, j: (i, 0, 0))
    return pl.pallas_call(
        functools.partial(_pool_prompt_kernel, tile=tile, w_a=w_a),
        grid=(b, s // tile),
        in_specs=[tok, mod, mod, mod, _const_spec((1, d)), _const_spec((1, d)),
                  _const_spec(in_w.shape), _const_spec(band.shape), _const_spec(bandh.shape),
                  _const_spec(grp_w.shape), _const_spec((1, w_a)), _const_spec(out_w.shape)],
        out_specs=[tok, pl.BlockSpec((1, HALO, w_a), lambda i, j: (i, 0, 0))],
        out_shape=[jax.ShapeDtypeStruct((b, s, d), F32),
                   jax.ShapeDtypeStruct((b, HALO, w_a), F32)],
        scratch_shapes=[pltpu.VMEM((HALO, w_a), BF16)],
        compiler_params=_params("arbitrary", "arbitrary"),
        name="pool_prompt",
    )(x, shift, scale, gate, pre_g, post_g, in_w, band, bandh, grp_w, a_scale, out_w)


def _pool_sample_kernel(x_ref, shift_ref, scale_ref, gate_ref, preg_ref, postg_ref, inw_ref,
                        hist_ref, grpw_ref, ascale_ref, outw_ref, x1_ref, u_ref, *, steps, w_a):
    gw = w_a // len(POOL_WINDOWS)
    rows = x_ref.shape[0]
    nh = hist_ref.shape[0]

    x = x_ref[...]
    h = (_rms(x) * preg_ref[...]) * (1.0 + scale_ref[...]) + shift_ref[...]
    uz = _dot(h.astype(BF16), inw_ref[...])
    u = uz[:, :w_a]
    z = uz[:, w_a:]
    u_ref[...] = u
    u_bf = u.astype(BF16)
    hist_bf = hist_ref[...].astype(BF16)

    n_batch = nh // POOL_BUF
    ro = lax.broadcasted_iota(jnp.int32, (rows, rows), 0)
    ri = lax.broadcasted_iota(jnp.int32, (rows, rows), 1)
    same = _div(ro, steps) == _div(ri, steps)
    lag_new = _mod(ro, steps) - _mod(ri, steps)
    roh = lax.broadcasted_iota(jnp.int32, (rows, nh), 0)
    ch = lax.broadcasted_iota(jnp.int32, (rows, nh), 1)
    in_batch = _mod(ch, n_batch) == _div(roh, steps)
    lag_hist = POOL_BUF + _mod(roh, steps) - _div(ch, n_batch)

    cols = []
    for g, w in enumerate(POOL_WINDOWS):
        c0 = g * gw
        sel_new = jnp.where(same & (lag_new >= 1) & (lag_new < w), 1.0, 0.0)
        sel_new = jnp.where(ro == ri, -(w - 1.0), sel_new).astype(BF16)
        sel_hist = jnp.where(in_batch & (lag_hist < w), 1.0, 0.0).astype(BF16)
        wsum = _dot(sel_new, u_bf[:, c0:c0 + gw]) + _dot(sel_hist, hist_bf[:, c0:c0 + gw])
        pooled = wsum / float(w)
        mixed = _dot(pooled.astype(BF16), grpw_ref[g]) * ascale_ref[:, c0:c0 + gw]
        cols.append((mixed * _silu(z[:, c0:c0 + gw])).astype(BF16))
    gated = jnp.concatenate(cols, axis=1)
    y = _dot(gated, outw_ref[...])
    x1_ref[...] = x + gate_ref[...] * (_rms(y) * postg_ref[...])


def _pool_sample(x, shift, scale, gate, pre_g, post_g, in_w, hist, grp_w, a_scale, out_w, *, steps):
    rows, d = x.shape
    w_a = out_w.shape[0]
    args = (x, shift, scale, gate, pre_g, post_g, in_w, hist, grp_w, a_scale, out_w)
    return pl.pallas_call(
        functools.partial(_pool_sample_kernel, steps=steps, w_a=w_a),
        grid=(1,),
        in_specs=[_const_spec(a.shape) for a in args],
        out_specs=[pl.BlockSpec((rows, d), lambda i: (0, 0)),
                   pl.BlockSpec((rows, w_a), lambda i: (0, 0))],
        out_shape=[jax.ShapeDtypeStruct((rows, d), F32),
                   jax.ShapeDtypeStruct((rows, w_a), F32)],
        compiler_params=_params("arbitrary"),
        name="pool_sample",
    )(*args)


def _proj_kernel(x_ref, shift_ref, scale_ref, kvg_ref, preg_ref, kvw_ref, wf_ref, fb_ref, binw_ref,
                 *rest, tile, d, prompt):
    if prompt:
        ltri_ref, place_ref, k_ref, vt_ref, sz_ref, kb_ref, vtb_ref, qb_ref, fb16_ref, lft_ref, carry_ref = rest
    else:
        k_ref, v_ref, sz_ref, logf_ref, q_ref = rest
    n_heads = d // HEAD_DIM

    r = _rms(x_ref[0])
    xn = (r * kvg_ref[...]).astype(BF16)
    h = ((r * preg_ref[...]) * (1.0 + scale_ref[0]) + shift_ref[0]).astype(BF16)

    kv = _dot(xn, kvw_ref[...])
    k = kv[:, :d]
    v = kv[:, d:]
    k_ref[0] = k
    logf = _log_sigmoid(_dot(xn, wf_ref[...]) + fb_ref[...])

    qz = _dot(h, binw_ref[...])
    sz_ref[0] = _silu(qz[:, d:]).astype(BF16)

    if prompt:
        q = qz[:, :d] * (HEAD_DIM ** -0.5 * LOG2E)
        vt = v.T
        vt_ref[0] = vt
        for hp in range(d // LANES):
            sl = slice(hp * LANES, (hp + 1) * LANES)
            kb_ref[0, hp] = k[:, sl].astype(BF16)
            vtb_ref[0, hp] = vt[sl].astype(BF16)
            qb_ref[0, hp] = q[:, sl].astype(BF16)

        @pl.when(pl.program_id(1) == 0)
        def _():
            carry_ref[...] = jnp.zeros_like(carry_ref)

        lft_ref[0] = logf.T[:n_heads]
        cum = _dot_exact_rhs(ltri_ref[...], logf) + carry_ref[...]
        carry_ref[...] = cum[tile - 1:tile]
        hi, mid, lo = _split3(cum * LOG2E)
        fb16_ref[0] = (_dot(hi, place_ref[0]) + _dot(mid, place_ref[1]) + _dot(lo, place_ref[2])).astype(BF16)
    else:
        v_ref[0] = v
        logf_ref[0] = logf[:, :n_heads]
        q_ref[0] = qz[:, :d] * (HEAD_DIM ** -0.5)


def _proj(x, shift, scale, kv_g, pre_g, kv_w_main, w_f, f_b, b_in_w, *, tile, prompt):
    b, s, d = x.shape
    n_heads = d // HEAD_DIM
    hp = d // LANES
    tok = pl.BlockSpec((1, tile, d), lambda i, j: (i, j, 0))
    mod_rows = shift.shape[1]
    mod_tile = tile if mod_rows == s else 1
    mod = pl.BlockSpec((1, mod_tile, d), (lambda i, j: (i, j, 0)) if mod_rows == s else (lambda i, j: (i, 0, 0)))
    in_specs = [tok, mod, mod, _const_spec((1, d)), _const_spec((1, d)), _const_spec(kv_w_main.shape),
                _const_spec(w_f.shape), _const_spec(f_b.shape), _const_spec(b_in_w.shape)]
    args = [x, shift, scale, kv_g, pre_g, kv_w_main, w_f, f_b, b_in_w]
    scratch = []
    if prompt:
        ltri = jnp.asarray(np.tril(np.ones((tile, tile), np.float32)), BF16)
        place = np.zeros((3, LANES, LANES), np.float32)
        for term in range(3):
            place[term, np.arange(n_heads), term * n_heads + np.arange(n_heads)] = 1.0
        place = jnp.asarray(place, BF16)
        in_specs += [_const_spec(ltri.shape), _const_spec(place.shape)]
        args += [ltri, place]
        pair = pl.BlockSpec((1, hp, tile, LANES), lambda i, j: (i, 0, j, 0))
        pair_t = pl.BlockSpec((1, hp, LANES, tile), lambda i, j: (i, 0, 0, j))
        out_specs = [tok, pl.BlockSpec((1, d, tile), lambda i, j: (i, 0, j)), tok, pair, pair_t, pair,
                     pl.BlockSpec((1, tile, LANES), lambda i, j: (i, j, 0)),
                     pl.BlockSpec((1, n_heads, tile), lambda i, j: (i, 0, j))]
        out_shape = [jax.ShapeDtypeStruct((b, s, d), F32), jax.ShapeDtypeStruct((b, d, s), F32),
                     jax.ShapeDtypeStruct((b, s, d), BF16), jax.ShapeDtypeStruct((b, hp, s, LANES), BF16),
                     jax.ShapeDtypeStruct((b, hp, LANES, s), BF16), jax.ShapeDtypeStruct((b, hp, s, LANES), BF16),
                     jax.ShapeDtypeStruct((b, s, LANES), BF16), jax.ShapeDtypeStruct((b, n_heads, s), F32)]
        scratch = [pltpu.VMEM((1, LANES), F32)]
    else:
        out_specs = [tok, tok, tok, pl.BlockSpec((1, tile, n_heads), lambda i, j: (i, j, 0)), tok]
        out_shape = [jax.ShapeDtypeStruct((b, s, d), F32), jax.ShapeDtypeStruct((b, s, d), F32),
                     jax.ShapeDtypeStruct((b, s, d), BF16), jax.ShapeDtypeStruct((b, s, n_heads), F32),
                     jax.ShapeDtypeStruct((b, s, d), F32)]
    return pl.pallas_call(
        functools.partial(_proj_kernel, tile=tile, d=d, prompt=prompt),
        grid=(b, s // tile),
        in_specs=in_specs,
        out_specs=out_specs,
        out_shape=out_shape,
        scratch_shapes=scratch,
        compiler_params=_params("arbitrary", "arbitrary"),
        name="proj_prompt" if prompt else "proj_sample",
    )(*args)


def _attn_kernel(q_ref, k_ref, f_ref, vt_ref, o_ref, *, tq, n_heads):
    pair = pl.program_id(1)
    qi = pl.program_id(2)
    q = q_ref[0, 0]
    lane = lax.broadcasted_iota(jnp.int32, (1, LANES), 1)
    zero = jnp.zeros_like(q)
    qa = []
    for hd in range(2):
        head = 2 * pair + hd
        mine = (lane >= hd * HEAD_DIM) & (lane < (hd + 1) * HEAD_DIM)
        pick = (lane == head) | (lane == n_heads + head) | (lane == 2 * n_heads + head)
        minus = jnp.where(pick, -1.0, 0.0).astype(BF16)
        qa.append(jnp.concatenate([jnp.where(mine, q, zero), jnp.broadcast_to(minus, (tq, LANES))], axis=1))

    def step(j, carry, diagonal):
        k0 = pl.multiple_of(j * tq, tq)
        ka = jnp.concatenate([k_ref[0, 0, pl.ds(k0, tq), :], f_ref[0, pl.ds(k0, tq), :]], axis=1)
        vt = vt_ref[0, 0, :, pl.ds(k0, tq)]
        out = []
        scores = [_dot_nt(ka, qa[hd]) for hd in range(2)]
        for hd in range(2):
            m, l, acc = carry[hd]
            st = scores[hd]
            if diagonal:
                key = lax.broadcasted_iota(jnp.int32, (tq, tq), 0)
                qry = lax.broadcasted_iota(jnp.int32, (tq, tq), 1)
                st = jnp.where(key <= qry, st, NEG)
            m_new = jnp.maximum(m, jnp.max(st, axis=0, keepdims=True))
            alpha = jnp.exp2(m - m_new)
            p = jnp.exp2(st - m_new)
            l = alpha * l + jnp.sum(p, axis=0, keepdims=True)
            acc = alpha * acc + _dot(vt[hd * HEAD_DIM:(hd + 1) * HEAD_DIM], p.astype(BF16))
            out.append((m_new, l, acc))
        return tuple(out)

    init = tuple((jnp.full((1, tq), NEG, F32), jnp.zeros((1, tq), F32), jnp.zeros((HEAD_DIM, tq), F32))
                 for _ in range(2))
    carry = lax.fori_loop(0, qi, functools.partial(step, diagonal=False), init)
    (_, l0, a0), (_, l1, a1) = step(qi, carry, True)
    o_ref[0] = jnp.concatenate([a0 / l0, a1 / l1], axis=0).T


def _attn_prompt(qb, kb, fb16, vtb, *, n_heads, tq=512):
    b, hp, s, _ = qb.shape
    return pl.pallas_call(
        functools.partial(_attn_kernel, tq=tq, n_heads=n_heads),
        grid=(b, hp, s // tq),
        in_specs=[pl.BlockSpec((1, 1, tq, LANES), lambda i, p, j: (i, p, j, 0)),
                  pl.BlockSpec((1, 1, s, LANES), lambda i, p, j: (i, p, 0, 0)),
                  pl.BlockSpec((1, s, LANES), lambda i, p, j: (i, 0, 0)),
                  pl.BlockSpec((1, 1, LANES, s), lambda i, p, j: (i, p, 0, 0))],
        out_specs=pl.BlockSpec((1, tq, LANES), lambda i, p, j: (i, j, p)),
        out_shape=jax.ShapeDtypeStruct((b, s, hp * LANES), F32),
        compiler_params=_params("arbitrary", "arbitrary", "arbitrary"),
        name="attn_prompt",
    )(qb, kb, fb16, vtb)


def _sample_attn_kernel(pt_ref, q_ref, knew_ref, vnew_ref, lfnew_ref, *rest, pages, steps, d):
    k_pages = rest[:pages]
    v_pages = rest[pages:2 * pages]
    f_pages = rest[2 * pages:3 * pages]
    o_ref, qbd_ref, acc_ref, m_ref, l_ref, fc_ref, knew_buf, vnew_buf, lfnew_buf = rest[3 * pages:]
    del pt_ref
    c = pl.program_id(1)
    n_heads = d // HEAD_DIM
    page = k_pages[0].shape[2]
    cols = n_heads * steps

    col_r = lax.broadcasted_iota(jnp.int32, (LANES, 1), 0)
    col_l = lax.broadcasted_iota(jnp.int32, (1, LANES), 1)

    def to_lanes(col):
        return jnp.broadcast_to(col, (LANES, LANES)).T[0:1, :]

    @pl.when((pl.program_id(0) == 0) & (c == 0))
    def _():
        knew_buf[...] = jnp.zeros_like(knew_buf)
        vnew_buf[...] = jnp.zeros_like(vnew_buf)
        lfnew_buf[...] = jnp.zeros_like(lfnew_buf)

    @pl.when(c == 0)
    def _():
        q = q_ref[0]
        lane_head = _div(lax.broadcasted_iota(jnp.int32, (LANES, d), 1), HEAD_DIM)
        row = lax.broadcasted_iota(jnp.int32, (LANES, d), 0)
        live = (row < cols) & (lane_head == _div(row, steps))
        qbd = jnp.zeros((LANES, d), F32)
        for i in range(steps):
            qbd = jnp.where(live & (_mod(row, steps) == i), q[i:i + 1, :], qbd)
        qbd_ref[...] = qbd.astype(BF16)
        m_ref[...] = jnp.full_like(m_ref, NEG)
        l_ref[...] = jnp.zeros_like(l_ref)
        acc_ref[...] = jnp.zeros_like(acc_ref)
        fc_ref[...] = jnp.zeros_like(fc_ref)

    expand = jnp.where((col_r < cols) & (lax.broadcasted_iota(jnp.int32, (LANES, n_heads), 1) == _div(col_r, steps)),
                       1.0, 0.0).astype(BF16)
    tri_r = lax.broadcasted_iota(jnp.int32, (BAND, BAND), 0)
    tri_c = lax.broadcasted_iota(jnp.int32, (BAND, BAND), 1)
    utri = jnp.where(tri_r <= tri_c, 1.0, 0.0).astype(BF16)

    def chunk(with_new):
        k_list = [r[0] for r in k_pages]
        v_list = [r[0] for r in v_pages]
        f_list = [r[0] for r in f_pages]
        if with_new:
            knew_buf[0:steps, :] = knew_ref[0]
            vnew_buf[0:steps, :] = vnew_ref[0]
            lfnew_buf[0:steps, 0:n_heads] = lfnew_ref[0]
            k_list.append(knew_buf[...].T)
            v_list.append(vnew_buf[...].T)
            f_list.append(lfnew_buf[...].T[:n_heads])
        kt = jnp.concatenate(k_list, axis=1).astype(BF16)
        vt = jnp.concatenate(v_list, axis=1).astype(BF16)
        lf = jnp.concatenate(f_list, axis=1)
        keys = kt.shape[1]

        st = _dot(qbd_ref[...], kt)
        lfe = _dot_exact_rhs(expand, lf)
        fc = fc_ref[...]
        cum = []
        for c0 in range(0, keys, BAND):
            n = min(BAND, keys - c0)
            blk = _dot_exact_lhs(lfe[:, c0:c0 + n], utri[:n, :n]) + fc
            fc = blk[:, n - 1:n]
            cum.append(blk)
        fc_ref[...] = fc
        st = st - jnp.concatenate(cum, axis=1)
        if with_new:
            key = lax.broadcasted_iota(jnp.int32, (1, keys), 1) - pages * page
            st = jnp.where((key < 0) | (key <= _mod(col_r, steps)), st, NEG)

        m = m_ref[...]
        m_new = jnp.maximum(m, jnp.max(st, axis=1, keepdims=True))
        alpha = jnp.exp(m - m_new)
        p = jnp.exp(st - m_new)
        l_ref[...] = alpha * l_ref[...] + jnp.sum(p, axis=1, keepdims=True)
        m_ref[...] = m_new
        acc_ref[...] = acc_ref[...] * to_lanes(alpha) + _dot(vt, p.T.astype(BF16))

    last = pl.num_programs(1) - 1

    @pl.when(c < last)
    def _():
        chunk(False)

    @pl.when(c == last)
    def _():
        chunk(True)
        res = acc_ref[...] / to_lanes(l_ref[...])
        row_head = _div(lax.broadcasted_iota(jnp.int32, (d, LANES), 0), HEAD_DIM)
        col = lax.broadcasted_iota(jnp.int32, (d, LANES), 1)
        res = jnp.where((col < cols) & (row_head == _div(col, steps)), res, 0.0)
        pick = jnp.where((col_l < cols) & (lax.broadcasted_iota(jnp.int32, (8, LANES), 0) == _mod(col_l, steps)),
                         1.0, 0.0).astype(BF16)
        a, b, c3 = _split3(res)
        o_ref[0] = (_dot_nt(pick, a) + _dot_nt(pick, b) + _dot_nt(pick, c3))[:steps]


def _attn_sample(page_table, q, k_new, v_new, lf_new, cache_k, cache_v, cache_logf, *, pages=8):
    nb, steps, d = q.shape
    n_pages = page_table.shape[1]
    n_phys, page, n_heads = cache_logf.shape
    cache_k = jnp.transpose(cache_k, (0, 2, 3, 1)).reshape(n_phys, d, page)
    cache_v = jnp.transpose(cache_v, (0, 2, 3, 1)).reshape(n_phys, d, page)
    cache_logf = jnp.transpose(cache_logf, (0, 2, 1))
    pt_flat = page_table.reshape(-1)

    def page_map(r):
        return lambda i, c, pt: (pt[i * n_pages + c * pages + r], 0, 0)

    row = lambda i, c, pt: (i, 0, 0)
    kv_specs = [pl.BlockSpec((1, d, page), page_map(r)) for r in range(pages)]
    f_specs = [pl.BlockSpec((1, n_heads, page), page_map(r)) for r in range(pages)]
    grid_spec = pltpu.PrefetchScalarGridSpec(
        num_scalar_prefetch=1,
        grid=(nb, n_pages // pages),
        in_specs=[pl.BlockSpec((1, steps, d), row), pl.BlockSpec((1, steps, d), row),
                  pl.BlockSpec((1, steps, d), row), pl.BlockSpec((1, steps, n_heads), row)]
                 + kv_specs + kv_specs + f_specs,
        out_specs=pl.BlockSpec((1, steps, d), row),
        scratch_shapes=[pltpu.VMEM((LANES, d), BF16), pltpu.VMEM((d, LANES), F32),
                        pltpu.VMEM((LANES, 1), F32), pltpu.VMEM((LANES, 1), F32),
                        pltpu.VMEM((LANES, 1), F32), pltpu.VMEM((LANES, d), F32),
                        pltpu.VMEM((LANES, d), F32), pltpu.VMEM((LANES, LANES), F32)],
    )
    return pl.pallas_call(
        functools.partial(_sample_attn_kernel, pages=pages, steps=steps, d=d),
        grid_spec=grid_spec,
        out_shape=jax.ShapeDtypeStruct((nb, steps, d), F32),
        compiler_params=_params("arbitrary", "arbitrary"),
        name="attn_sample",
    )(pt_flat, q, k_new, v_new, lf_new, *([cache_k] * pages), *([cache_v] * pages), *([cache_logf] * pages))


def _outproj_kernel(o_ref, sz_ref, w_ref, postg_ref, gate_ref, x_ref, y_ref):
    g = (o_ref[0] * sz_ref[0].astype(F32)).astype(BF16)
    out = _dot(g, w_ref[...])
    y_ref[0] = x_ref[0] + gate_ref[0] * (_rms(out) * postg_ref[...])


def _outproj(o, sz, w, post_g, gate, x, *, tile, name):
    b, s, d = x.shape
    tok = pl.BlockSpec((1, tile, d), lambda i, j: (i, j, 0))
    per_row = gate.shape[1] == s
    mod = pl.BlockSpec((1, tile if per_row else 1, d),
                       (lambda i, j: (i, j, 0)) if per_row else (lambda i, j: (i, 0, 0)))
    return pl.pallas_call(
        _outproj_kernel,
        grid=(b, s // tile),
        in_specs=[tok, tok, _const_spec(w.shape), _const_spec((1, d)), mod, tok],
        out_specs=tok,
        out_shape=jax.ShapeDtypeStruct((b, s, d), F32),
        compiler_params=_params("arbitrary", "arbitrary"),
        name=name,
    )(o, sz, w, post_g, gate, x)


def kernel(x_prompt, x_sample, state_pool, cache_k, cache_v, cache_logf, page_table, c_prompt, c_sample,
           ada_w, ada_b, pre_g, post_g, a_in_w, a_grp_w, a_scale, a_out_w, kv_g, kv_w, f_b, b_in_w, b_out_w):
    bp, seq, d = x_prompt.shape
    bs, steps, _ = x_sample.shape
    n_heads = d // HEAD_DIM
    w_a = a_out_w.shape[1]
    rows_s = bs * steps

    pad = (-(bp + bs)) % 8
    c_all = jnp.concatenate([c_prompt, c_sample, jnp.zeros((pad, d), F32)], axis=0)
    mod = _adaln(c_all, ada_w, ada_b)

    def mods(layer):
        m = mod[layer]
        parts = [m[:, i * d:(i + 1) * d] for i in range(3)]
        prompt = [p[:bp].reshape(bp, 1, d) for p in parts]
        sample = [jnp.repeat(p[bp:bp + bs], steps, axis=0) for p in parts]
        return prompt, sample

    (shift0_p, scale0_p, gate0_p), (shift0_s, scale0_s, gate0_s) = mods(0)
    (shift1_p, scale1_p, gate1_p), (shift1_s, scale1_s, gate1_s) = mods(1)

    in_w = a_in_w[0].astype(BF16)
    grp_w = a_grp_w[0].astype(BF16)
    out_w = a_out_w[0].astype(BF16)
    kv_w_main = kv_w[:, :2 * d].astype(BF16)
    w_f = jnp.pad(kv_w[:, 2 * d:], ((0, 0), (0, LANES - n_heads))).astype(BF16)
    f_b_pad = jnp.pad(f_b, (0, LANES - n_heads)).reshape(1, LANES)
    bin_w = b_in_w[0].astype(BF16)
    bout_w = b_out_w[0].astype(BF16)
    pre0, pre1 = pre_g[0].reshape(1, d), pre_g[1].reshape(1, d)
    post0, post1 = post_g[0].reshape(1, d), post_g[1].reshape(1, d)
    kvg = kv_g.reshape(1, d)
    asc = a_scale[0].reshape(1, w_a)

    x1_p, tail_p = _pool_prompt(x_prompt, shift0_p, scale0_p, gate0_p, pre0, post0, in_w, grp_w, asc, out_w)
    pool_prompt = tail_p[None, :, HALO - POOL_BUF:, :]
    k_p, vt_p, sz_p, kb, vtb, qb, fb16, lft_p = _proj(
        x1_p, shift1_p, scale1_p, kvg, pre1, kv_w_main, w_f, f_b_pad, bin_w, tile=512, prompt=True)
    logf_p = jnp.transpose(lft_p, (0, 2, 1))
    v_p = jnp.transpose(vt_p.reshape(bp, n_heads, HEAD_DIM, seq), (0, 3, 1, 2))
    o_p = _attn_prompt(qb, kb, fb16, vtb, n_heads=n_heads)
    y_prompt = _outproj(o_p, sz_p, bout_w, post1, gate1_p, x1_p, tile=512, name="outproj_prompt")

    hist = jnp.transpose(state_pool[0], (1, 0, 2)).reshape(POOL_BUF * bs, w_a)
    x1_s, u_s = _pool_sample(x_sample.reshape(rows_s, d), shift0_s, scale0_s, gate0_s, pre0, post0,
                             in_w, hist, grp_w, asc, out_w, steps=steps)
    pool_sample = jnp.concatenate([state_pool[:, :, steps:], u_s.reshape(1, bs, steps, w_a)], axis=2)
    x1_s3 = x1_s.reshape(1, rows_s, d)
    k_s, v_s, sz_s, logf_s, q_s = _proj(
        x1_s3, shift1_s[None], scale1_s[None], kvg, pre1, kv_w_main, w_f, f_b_pad, bin_w,
        tile=rows_s, prompt=False)
    o_s = _attn_sample(page_table, q_s.reshape(bs, steps, d), k_s.reshape(bs, steps, d),
                       v_s.reshape(bs, steps, d), logf_s.reshape(bs, steps, n_heads),
                       cache_k, cache_v, cache_logf)
    y_sample = _outproj(o_s.reshape(1, rows_s, d), sz_s, bout_w, post1, gate1_s[None], x1_s3,
                        tile=rows_s, name="outproj_sample")

    return (y_prompt, y_sample.reshape(bs, steps, d), pool_prompt, pool_sample,
            k_p.reshape(bp, seq, n_heads, HEAD_DIM), v_p, logf_p,
            k_s.reshape(bs, steps, n_heads, HEAD_DIM), v_s.reshape(bs, steps, n_heads, HEAD_DIM),
            logf_s.reshape(bs, steps, n_heads))
```

```python
import functools

import jax
import jax.numpy as jnp
import numpy as np
from jax import lax
from jax.experimental import pallas as pl
from jax.experimental.pallas import tpu as pltpu

F32 = jnp.float32
BF16 = jnp.bfloat16

EPS = 1e-6
NEG = -1e30
LOG2E = 1.4426950408889634
POOL_WINDOWS = (2, 4, 8, 16)
POOL_BUF = max(POOL_WINDOWS) - 1
HEAD_DIM = 64
HALO = 16

V7X_VMEM_BYTES = 64 * 1024 * 1024
VMEM_LIMIT = V7X_VMEM_BYTES - 8 * 1024 * 1024
LANES = 128
BAND = 256
STRIP = 32


def _params(*sem):
    return pltpu.CompilerParams(dimension_semantics=sem, vmem_limit_bytes=VMEM_LIMIT)


def _const_spec(shape):
    nd = len(shape)
    return pl.BlockSpec(shape, lambda *_: (0,) * nd, pipeline_mode=pl.Buffered(1))


def _silu(x):
    return x * jax.nn.sigmoid(x)


def _rms(x):
    return x * lax.rsqrt(jnp.mean(x * x, axis=-1, keepdims=True) + EPS)


def _split3(x):
    a = x.astype(BF16)
    r = x - a.astype(F32)
    b = r.astype(BF16)
    c = (r - b.astype(F32)).astype(BF16)
    return a, b, c


def _log_sigmoid(x):
    return jnp.minimum(x, 0.0) - jnp.log1p(jnp.exp(-jnp.abs(x)))


def _div(x, n):
    return x >> (n.bit_length() - 1) if n & (n - 1) == 0 else x // n


def _mod(x, n):
    return x & (n - 1) if n & (n - 1) == 0 else x % n


def _dot(a, b):
    return jnp.dot(a, b, preferred_element_type=F32)


def _dot_nt(a, b):
    return lax.dot_general(a, b, (((1,), (1,)), ((), ())), preferred_element_type=F32)


def _dot_exact_rhs(sel, x):
    a, b, c = _split3(x)
    return _dot(sel, a) + _dot(sel, b) + _dot(sel, c)


def _dot_exact_lhs(x, sel):
    a, b, c = _split3(x)
    return _dot(a, sel) + _dot(b, sel) + _dot(c, sel)


def _adaln_kernel(c_ref, w_ref, b_ref, o_ref):
    a = _silu(c_ref[...]).astype(BF16)
    o_ref[0] = _dot(a, w_ref[0].astype(BF16)) + b_ref[0]


def _adaln(c_all, ada_w, ada_b):
    depth, d, d3 = ada_w.shape
    rows = c_all.shape[0]
    tn = d3 // 2
    return pl.pallas_call(
        _adaln_kernel,
        grid=(depth, d3 // tn),
        in_specs=[
            pl.BlockSpec((rows, d), lambda l, j: (0, 0)),
            pl.BlockSpec((1, d, tn), lambda l, j: (l, 0, j)),
            pl.BlockSpec((1, 1, tn), lambda l, j: (l, 0, j)),
        ],
        out_specs=pl.BlockSpec((1, rows, tn), lambda l, j: (l, 0, j)),
        out_shape=jax.ShapeDtypeStruct((depth, rows, d3), F32),
        compiler_params=_params("arbitrary", "arbitrary"),
        name="adaln",
    )(c_all, ada_w, ada_b.reshape(depth, 1, d3))


def _band_constants():
    t = np.arange(BAND)[:, None]
    s = np.arange(BAND)[None, :]
    eye = (t == s).astype(np.float32)

    def one(w, first):
        cnt = np.minimum(t + 1, w) if first else w
        return ((t - s >= 1) & (t - s < w)).astype(np.float32) - (cnt - 1) * eye

    band = np.stack([np.stack([one(w, first) for w in POOL_WINDOWS]) for first in (True, False)])
    th = np.arange(HALO)[:, None]
    jh = np.arange(HALO)[None, :]
    bandh = np.stack([(th - jh + HALO < w) for w in POOL_WINDOWS]).astype(np.float32)
    return jnp.asarray(band, BF16), jnp.asarray(bandh, BF16)


def _pool_prompt_kernel(x_ref, shift_ref, scale_ref, gate_ref, preg_ref, postg_ref, inw_ref,
                        band_ref, bandh_ref, grpw_ref, ascale_ref, outw_ref,
                        x1_ref, tail_ref, halo_ref, *, tile, w_a):
    t = pl.program_id(1)
    gw = w_a // len(POOL_WINDOWS)

    @pl.when(t == 0)
    def _():
        halo_ref[...] = jnp.zeros_like(halo_ref)

    x = x_ref[0]
    h = (_rms(x) * preg_ref[...]) * (1.0 + scale_ref[0]) + shift_ref[0]
    uz = _dot(h.astype(BF16), inw_ref[...])
    u = uz[:, :w_a]
    z = uz[:, w_a:]
    u_bf = u.astype(BF16)

    gated_blocks = []
    for blk in range(tile // BAND):
        r0 = blk * BAND
        halo = halo_ref[...] if blk == 0 else u_bf[r0 - HALO:r0]
        pos = t * tile + r0 + lax.broadcasted_iota(jnp.int32, (BAND, 1), 0)
        kind = jnp.where(t == 0, 0, 1) if blk == 0 else 1
        cols = []
        for g, w in enumerate(POOL_WINDOWS):
            c0 = g * gw
            ug = u_bf[r0:r0 + BAND, c0:c0 + gw]
            wsum = _dot(band_ref[kind, g], ug)
            top = wsum[:HALO] + _dot(bandh_ref[g], halo[:, c0:c0 + gw])
            wsum = jnp.concatenate([top, wsum[HALO:]], axis=0)
            cnt = jnp.minimum(pos + 1, w).astype(F32)
            pooled = wsum / cnt
            mixed = _dot(pooled.astype(BF16), grpw_ref[g]) * ascale_ref[:, c0:c0 + gw]
            cols.append((mixed * _silu(z[r0:r0 + BAND, c0:c0 + gw])).astype(BF16))
        gated_blocks.append(jnp.concatenate(cols, axis=1))
    gated = gated_blocks[0] if len(gated_blocks) == 1 else jnp.concatenate(gated_blocks, axis=0)

    y = _dot(gated, outw_ref[...])
    x1_ref[0] = x + gate_ref[0] * (_rms(y) * postg_ref[...])

    halo_ref[...] = u_bf[tile - HALO:]

    @pl.when(t == pl.num_programs(1) - 1)
    def _():
        tail_ref[0] = u[tile - HALO:]


def _pool_prompt(x, shift, scale, gate, pre_g, post_g, in_w, grp_w, a_scale, out_w, *, tile=256):
    b, s, d = x.shape
    w_a = out_w.shape[0]
    band, bandh = _band_constants()
    tok = pl.BlockSpec((1, tile, d), lambda i, j: (i, j, 0))
    mod = pl.BlockSpec((1, 1, d), lambda i, j: (i, 0, 0))
    return pl.pallas_call(
        functools.partial(_pool_prompt_kernel, tile=tile, w_a=w_a),
        grid=(b, s // tile),
        in_specs=[tok, mod, mod, mod, _const_spec((1, d)), _const_spec((1, d)),
                  _const_spec(in_w.shape), _const_spec(band.shape), _const_spec(bandh.shape),
                  _const_spec(grp_w.shape), _const_spec((1, w_a)), _const_spec(out_w.shape)],
        out_specs=[tok, pl.BlockSpec((1, HALO, w_a), lambda i, j: (i, 0, 0))],
        out_shape=[jax.ShapeDtypeStruct((b, s, d), F32),
                   jax.ShapeDtypeStruct((b, HALO, w_a), F32)],
        scratch_shapes=[pltpu.VMEM((HALO, w_a), BF16)],
        compiler_params=_params("arbitrary", "arbitrary"),
        name="pool_prompt",
    )(x, shift, scale, gate, pre_g, post_g, in_w, band, bandh, grp_w, a_scale, out_w)


def _pool_sample_kernel(x_ref, shift_ref, scale_ref, gate_ref, preg_ref, postg_ref, inw_ref,
                        hist_ref, grpw_ref, ascale_ref, outw_ref, x1_ref, u_ref, *, steps, w_a):
    gw = w_a // len(POOL_WINDOWS)
    rows = x_ref.shape[0]
    nh = hist_ref.shape[0]

    x = x_ref[...]
    h = (_rms(x) * preg_ref[...]) * (1.0 + scale_ref[...]) + shift_ref[...]
    uz = _dot(h.astype(BF16), inw_ref[...])
    u = uz[:, :w_a]
    z = uz[:, w_a:]
    u_ref[...] = u
    u_bf = u.astype(BF16)
    hist_bf = hist_ref[...].astype(BF16)

    n_batch = nh // POOL_BUF
    ro = lax.broadcasted_iota(jnp.int32, (rows, rows), 0)
    ri = lax.broadcasted_iota(jnp.int32, (rows, rows), 1)
    same = _div(ro, steps) == _div(ri, steps)
    lag_new = _mod(ro, steps) - _mod(ri, steps)
    roh = lax.broadcasted_iota(jnp.int32, (rows, nh), 0)
    ch = lax.broadcasted_iota(jnp.int32, (rows, nh), 1)
    in_batch = _mod(ch, n_batch) == _div(roh, steps)
    lag_hist = POOL_BUF + _mod(roh, steps) - _div(ch, n_batch)

    cols = []
    for g, w in enumerate(POOL_WINDOWS):
        c0 = g * gw
        sel_new = jnp.where(same & (lag_new >= 1) & (lag_new < w), 1.0, 0.0)
        sel_new = jnp.where(ro == ri, -(w - 1.0), sel_new).astype(BF16)
        sel_hist = jnp.where(in_batch & (lag_hist < w), 1.0, 0.0).astype(BF16)
        wsum = _dot(sel_new, u_bf[:, c0:c0 + gw]) + _dot(sel_hist, hist_bf[:, c0:c0 + gw])
        pooled = wsum / float(w)
        mixed = _dot(pooled.astype(BF16), grpw_ref[g]) * ascale_ref[:, c0:c0 + gw]
        cols.append((mixed * _silu(z[:, c0:c0 + gw])).astype(BF16))
    gated = jnp.concatenate(cols, axis=1)
    y = _dot(gated, outw_ref[...])
    x1_ref[...] = x + gate_ref[...] * (_rms(y) * postg_ref[...])


def _pool_sample(x, shift, scale, gate, pre_g, post_g, in_w, hist, grp_w, a_scale, out_w, *, steps):
    rows, d = x.shape
    w_a = out_w.shape[0]
    args = (x, shift, scale, gate, pre_g, post_g, in_w, hist, grp_w, a_scale, out_w)
    return pl.pallas_call(
        functools.partial(_pool_sample_kernel, steps=steps, w_a=w_a),
        grid=(1,),
        in_specs=[_const_spec(a.shape) for a in args],
        out_specs=[pl.BlockSpec((rows, d), lambda i: (0, 0)),
                   pl.BlockSpec((rows, w_a), lambda i: (0, 0))],
        out_shape=[jax.ShapeDtypeStruct((rows, d), F32),
                   jax.ShapeDtypeStruct((rows, w_a), F32)],
        compiler_params=_params("arbitrary"),
        name="pool_sample",
    )(*args)


def _proj_kernel(x_ref, shift_ref, scale_ref, kvg_ref, preg_ref, kvw_ref, wf_ref, fb_ref, binw_ref,
                 *rest, tile, d, prompt):
    if prompt:
        ltri_ref, place_ref, k_ref, vt_ref, sz_ref, kb_ref, vtb_ref, qtb_ref, fb16_ref, lft_ref, carry_ref = rest
    else:
        k_ref, v_ref, sz_ref, logf_ref, q_ref = rest
    n_heads = d // HEAD_DIM

    r = _rms(x_ref[0])
    xn = (r * kvg_ref[...]).astype(BF16)
    h = ((r * preg_ref[...]) * (1.0 + scale_ref[0]) + shift_ref[0]).astype(BF16)

    kv = _dot(xn, kvw_ref[...])
    k = kv[:, :d]
    v = kv[:, d:]
    k_ref[0] = k
    logf = _log_sigmoid(_dot(xn, wf_ref[...]) + fb_ref[...])

    qz = _dot(h, binw_ref[...])
    sz_ref[0] = _silu(qz[:, d:]).astype(BF16)

    if prompt:
        qt = (qz[:, :d] * (HEAD_DIM ** -0.5 * LOG2E)).T
        vt = v.T
        vt_ref[0] = vt
        for hp in range(d // LANES):
            sl = slice(hp * LANES, (hp + 1) * LANES)
            kb_ref[0, hp] = k[:, sl].astype(BF16)
            vtb_ref[0, hp] = vt[sl].astype(BF16)
            qtb_ref[0, hp] = qt[sl].astype(BF16)

        @pl.when(pl.program_id(1) == 0)
        def _():
            carry_ref[...] = jnp.zeros_like(carry_ref)

        lft_ref[0] = logf.T[:n_heads]
        cum = _dot_exact_rhs(ltri_ref[...], logf) + carry_ref[...]
        carry_ref[...] = cum[tile - 1:tile]
        hi, mid, lo = _split3(cum * LOG2E)
        fb16_ref[0] = (_dot(hi, place_ref[0]) + _dot(mid, place_ref[1]) + _dot(lo, place_ref[2])).astype(BF16)
    else:
        v_ref[0] = v
        logf_ref[0] = logf[:, :n_heads]
        q_ref[0] = qz[:, :d] * (HEAD_DIM ** -0.5)


def _proj(x, shift, scale, kv_g, pre_g, kv_w_main, w_f, f_b, b_in_w, *, tile, prompt):
    b, s, d = x.shape
    n_heads = d // HEAD_DIM
    hp = d // LANES
    tok = pl.BlockSpec((1, tile, d), lambda i, j: (i, j, 0))
    mod_rows = shift.shape[1]
    mod_tile = tile if mod_rows == s else 1
    mod = pl.BlockSpec((1, mod_tile, d), (lambda i, j: (i, j, 0)) if mod_rows == s else (lambda i, j: (i, 0, 0)))
    in_specs = [tok, mod, mod, _const_spec((1, d)), _const_spec((1, d)), _const_spec(kv_w_main.shape),
                _const_spec(w_f.shape), _const_spec(f_b.shape), _const_spec(b_in_w.shape)]
    args = [x, shift, scale, kv_g, pre_g, kv_w_main, w_f, f_b, b_in_w]
    scratch = []
    if prompt:
        ltri = jnp.asarray(np.tril(np.ones((tile, tile), np.float32)), BF16)
        place = np.zeros((3, LANES, LANES), np.float32)
        for term in range(3):
            place[term, np.arange(n_heads), term * n_heads + np.arange(n_heads)] = 1.0
        place = jnp.asarray(place, BF16)
        in_specs += [_const_spec(ltri.shape), _const_spec(place.shape)]
        args += [ltri, place]
        pair = pl.BlockSpec((1, hp, tile, LANES), lambda i, j: (i, 0, j, 0))
        pair_t = pl.BlockSpec((1, hp, LANES, tile), lambda i, j: (i, 0, 0, j))
        out_specs = [tok, pl.BlockSpec((1, d, tile), lambda i, j: (i, 0, j)), tok, pair, pair_t, pair_t,
                     pl.BlockSpec((1, tile, LANES), lambda i, j: (i, j, 0)),
                     pl.BlockSpec((1, n_heads, tile), lambda i, j: (i, 0, j))]
        out_shape = [jax.ShapeDtypeStruct((b, s, d), F32), jax.ShapeDtypeStruct((b, d, s), F32),
                     jax.ShapeDtypeStruct((b, s, d), BF16), jax.ShapeDtypeStruct((b, hp, s, LANES), BF16),
                     jax.ShapeDtypeStruct((b, hp, LANES, s), BF16), jax.ShapeDtypeStruct((b, hp, LANES, s), BF16),
                     jax.ShapeDtypeStruct((b, s, LANES), BF16), jax.ShapeDtypeStruct((b, n_heads, s), F32)]
        scratch = [pltpu.VMEM((1, LANES), F32)]
    else:
        out_specs = [tok, tok, tok, pl.BlockSpec((1, tile, n_heads), lambda i, j: (i, j, 0)), tok]
        out_shape = [jax.ShapeDtypeStruct((b, s, d), F32), jax.ShapeDtypeStruct((b, s, d), F32),
                     jax.ShapeDtypeStruct((b, s, d), BF16), jax.ShapeDtypeStruct((b, s, n_heads), F32),
                     jax.ShapeDtypeStruct((b, s, d), F32)]
    return pl.pallas_call(
        functools.partial(_proj_kernel, tile=tile, d=d, prompt=prompt),
        grid=(b, s // tile),
        in_specs=in_specs,
        out_specs=out_specs,
        out_shape=out_shape,
        scratch_shapes=scratch,
        compiler_params=_params("arbitrary", "arbitrary"),
        name="proj_prompt" if prompt else "proj_sample",
    )(*args)


def _attn_kernel(qt_ref, k_ref, f_ref, vt_ref, o_ref, sa_ref, sb_ref, pa_ref, pb_ref, *, tq, n_heads):
    tk = tq
    pair = pl.program_id(1)
    qi = pl.program_id(2)
    qt = qt_ref[0, 0]
    row = lax.broadcasted_iota(jnp.int32, (LANES, 1), 0)
    zero = jnp.zeros_like(qt)
    qat = []
    for hd in range(2):
        head = 2 * pair + hd
        mine = (row >= hd * HEAD_DIM) & (row < (hd + 1) * HEAD_DIM)
        pick = (row == head) | (row == n_heads + head) | (row == 2 * n_heads + head)
        minus = jnp.where(pick, -1.0, 0.0).astype(BF16)
        qat.append(jnp.concatenate([jnp.where(mine, qt, zero), jnp.broadcast_to(minus, (LANES, tq))], axis=0))

    def qk(tile, s_ref):
        k0 = pl.multiple_of(tile * tk, tk)
        ka = jnp.concatenate([k_ref[0, 0, pl.ds(k0, tk), :], f_ref[0, pl.ds(k0, tk), :]], axis=1)
        for hd in range(2):
            s_ref[hd] = _dot(ka, qat[hd])

    ones = jnp.ones((8, tk), BF16)

    def pv(tile, p_ref):
        k0 = pl.multiple_of(tile * tk, tk)
        vt = vt_ref[0, 0, :, pl.ds(k0, tk)]
        return [_dot(jnp.concatenate([vt[hd * HEAD_DIM:(hd + 1) * HEAD_DIM], ones], axis=0), p_ref[hd])
                for hd in range(2)]

    def softmax(s_ref, p_ref, stats, diagonal=False):
        new_stats, alphas = [], []

        def scores(hd, r0):
            st = s_ref[hd, r0:r0 + STRIP, :]
            if diagonal:
                key = lax.broadcasted_iota(jnp.int32, (STRIP, tq), 0) + r0
                qry = lax.broadcasted_iota(jnp.int32, (STRIP, tq), 1)
                st = jnp.where(key <= qry, st, NEG)
            return st

        def fold(x, op):
            return op(x.reshape(STRIP // 8, 8, tq), axis=0)

        for hd in range(2):
            m = stats[hd]
            top = jnp.full((8, tq), NEG, F32)
            for r0 in range(0, tk, STRIP):
                top = jnp.maximum(top, fold(scores(hd, r0), jnp.max))
            m_new = jnp.maximum(m, jnp.max(top, axis=0, keepdims=True))
            alphas.append(jnp.exp2(m - m_new))
            for r0 in range(0, tk, STRIP):
                p_ref[hd, r0:r0 + STRIP, :] = jnp.exp2((scores(hd, r0) - m_new).astype(BF16))
            new_stats.append(m_new)
        return new_stats, alphas

    def rescale_add(alpha, acc, part):
        return [alpha[hd] * acc[hd] + part[hd] for hd in range(2)]

    qk(0, sa_ref)
    pb_ref[...] = jnp.zeros_like(pb_ref)
    stats = [jnp.full((1, tq), NEG, F32) for _ in range(2)]
    alpha_b = [jnp.ones((1, tq), F32) for _ in range(2)]
    acc = [jnp.zeros((HEAD_DIM + 8, tq), F32) for _ in range(2)]

    def body(t, carry):
        stats, alpha_b, acc = carry
        a = 2 * t
        part = pv(jnp.maximum(a - 1, 0), pb_ref)
        qk(a + 1, sb_ref)
        stats, alpha_a = softmax(sa_ref, pa_ref, stats)
        acc = rescale_add(alpha_b, acc, part)
        part = pv(a, pa_ref)
        qk(a + 2, sa_ref)
        stats, alpha_b = softmax(sb_ref, pb_ref, stats)
        acc = rescale_add(alpha_a, acc, part)
        return stats, alpha_b, acc

    pairs = qi // 2
    stats, alpha_b, acc = lax.fori_loop(0, pairs, body, (stats, alpha_b, acc))
    a = 2 * pairs

    def finish(acc):
        o_ref[0] = jnp.concatenate(
            [acc[hd][:HEAD_DIM] / acc[hd][HEAD_DIM:HEAD_DIM + 1] for hd in range(2)], axis=0).T

    @pl.when(a == qi)
    def _():
        part = pv(jnp.maximum(a - 1, 0), pb_ref)
        _, alpha_a = softmax(sa_ref, pa_ref, stats, diagonal=True)
        out = rescale_add(alpha_b, acc, part)
        finish(rescale_add(alpha_a, out, pv(a, pa_ref)))

    @pl.when(a != qi)
    def _():
        part = pv(jnp.maximum(a - 1, 0), pb_ref)
        qk(a + 1, sb_ref)
        mid, alpha_a = softmax(sa_ref, pa_ref, stats)
        out = rescale_add(alpha_b, acc, part)
        part = pv(a, pa_ref)
        _, alpha_d = softmax(sb_ref, pb_ref, mid, diagonal=True)
        out = rescale_add(alpha_a, out, part)
        finish(rescale_add(alpha_d, out, pv(a + 1, pb_ref)))


def _attn_prompt(qtb, kb, fb16, vtb, *, n_heads, tq=512):
    b, hp, s, _ = kb.shape
    tk = tq
    return pl.pallas_call(
        functools.partial(_attn_kernel, tq=tq, n_heads=n_heads),
        grid=(b, hp, s // tq),
        in_specs=[pl.BlockSpec((1, 1, LANES, tq), lambda i, p, j: (i, p, 0, j)),
                  pl.BlockSpec((1, 1, s, LANES), lambda i, p, j: (i, p, 0, 0)),
                  pl.BlockSpec((1, s, LANES), lambda i, p, j: (i, 0, 0)),
                  pl.BlockSpec((1, 1, LANES, s), lambda i, p, j: (i, p, 0, 0))],
        out_specs=pl.BlockSpec((1, tq, LANES), lambda i, p, j: (i, j, p)),
        out_shape=jax.ShapeDtypeStruct((b, s, hp * LANES), F32),
        scratch_shapes=[pltpu.VMEM((2, tk, tq), F32), pltpu.VMEM((2, tk, tq), F32),
                        pltpu.VMEM((2, tk, tq), BF16), pltpu.VMEM((2, tk, tq), BF16)],
        compiler_params=_params("arbitrary", "arbitrary", "arbitrary"),
        name="attn_prompt",
    )(qtb, kb, fb16, vtb)


def _sample_attn_kernel(pt_ref, q_ref, knew_ref, vnew_ref, lfnew_ref, *rest, pages, steps, d):
    k_pages = rest[:pages]
    v_pages = rest[pages:2 * pages]
    f_pages = rest[2 * pages:3 * pages]
    o_ref, qbd_ref, acc_ref, m_ref, l_ref, fc_ref, knew_buf, vnew_buf, lfnew_buf = rest[3 * pages:]
    del pt_ref
    c = pl.program_id(1)
    n_heads = d // HEAD_DIM
    page = k_pages[0].shape[2]
    cols = n_heads * steps

    col_r = lax.broadcasted_iota(jnp.int32, (LANES, 1), 0)
    col_l = lax.broadcasted_iota(jnp.int32, (1, LANES), 1)

    def to_lanes(col):
        return jnp.broadcast_to(col, (LANES, LANES)).T[0:1, :]

    @pl.when((pl.program_id(0) == 0) & (c == 0))
    def _():
        knew_buf[...] = jnp.zeros_like(knew_buf)
        vnew_buf[...] = jnp.zeros_like(vnew_buf)
        lfnew_buf[...] = jnp.zeros_like(lfnew_buf)

    @pl.when(c == 0)
    def _():
        q = q_ref[0]
        lane_head = _div(lax.broadcasted_iota(jnp.int32, (LANES, d), 1), HEAD_DIM)
        row = lax.broadcasted_iota(jnp.int32, (LANES, d), 0)
        live = (row < cols) & (lane_head == _div(row, steps))
        qbd = jnp.zeros((LANES, d), F32)
        for i in range(steps):
            qbd = jnp.where(live & (_mod(row, steps) == i), q[i:i + 1, :], qbd)
        qbd_ref[...] = qbd.astype(BF16)
        m_ref[...] = jnp.full_like(m_ref, NEG)
        l_ref[...] = jnp.zeros_like(l_ref)
        acc_ref[...] = jnp.zeros_like(acc_ref)
        fc_ref[...] = jnp.zeros_like(fc_ref)

    expand = jnp.where((col_r < cols) & (lax.broadcasted_iota(jnp.int32, (LANES, n_heads), 1) == _div(col_r, steps)),
                       1.0, 0.0).astype(BF16)
    tri_r = lax.broadcasted_iota(jnp.int32, (BAND, BAND), 0)
    tri_c = lax.broadcasted_iota(jnp.int32, (BAND, BAND), 1)
    utri = jnp.where(tri_r <= tri_c, 1.0, 0.0).astype(BF16)

    def chunk(with_new):
        k_list = [r[0] for r in k_pages]
        v_list = [r[0] for r in v_pages]
        f_list = [r[0] for r in f_pages]
        if with_new:
            knew_buf[0:steps, :] = knew_ref[0]
            vnew_buf[0:steps, :] = vnew_ref[0]
            lfnew_buf[0:steps, 0:n_heads] = lfnew_ref[0]
            k_list.append(knew_buf[...].T)
            v_list.append(vnew_buf[...].T)
            f_list.append(lfnew_buf[...].T[:n_heads])
        kt = jnp.concatenate(k_list, axis=1).astype(BF16)
        vt = jnp.concatenate(v_list, axis=1).astype(BF16)
        lf = jnp.concatenate(f_list, axis=1)
        keys = kt.shape[1]

        st = _dot(qbd_ref[...], kt)
        lfe = _dot_exact_rhs(expand, lf)
        fc = fc_ref[...]
        cum = []
        for c0 in range(0, keys, BAND):
            n = min(BAND, keys - c0)
            blk = _dot_exact_lhs(lfe[:, c0:c0 + n], utri[:n, :n]) + fc
            fc = blk[:, n - 1:n]
            cum.append(blk)
        fc_ref[...] = fc
        st = st - jnp.concatenate(cum, axis=1)
        if with_new:
            key = lax.broadcasted_iota(jnp.int32, (1, keys), 1) - pages * page
            st = jnp.where((key < 0) | (key <= _mod(col_r, steps)), st, NEG)

        m = m_ref[...]
        m_new = jnp.maximum(m, jnp.max(st, axis=1, keepdims=True))
        alpha = jnp.exp(m - m_new)
        p = jnp.exp(st - m_new)
        l_ref[...] = alpha * l_ref[...] + jnp.sum(p, axis=1, keepdims=True)
        m_ref[...] = m_new
        acc_ref[...] = acc_ref[...] * to_lanes(alpha) + _dot(vt, p.T.astype(BF16))

    last = pl.num_programs(1) - 1

    @pl.when(c < last)
    def _():
        chunk(False)

    @pl.when(c == last)
    def _():
        chunk(True)
        res = acc_ref[...] / to_lanes(l_ref[...])
        row_head = _div(lax.broadcasted_iota(jnp.int32, (d, LANES), 0), HEAD_DIM)
        col = lax.broadcasted_iota(jnp.int32, (d, LANES), 1)
        res = jnp.where((col < cols) & (row_head == _div(col, steps)), res, 0.0)
        pick = jnp.where((col_l < cols) & (lax.broadcasted_iota(jnp.int32, (8, LANES), 0) == _mod(col_l, steps)),
                         1.0, 0.0).astype(BF16)
        a, b, c3 = _split3(res)
        o_ref[0] = (_dot_nt(pick, a) + _dot_nt(pick, b) + _dot_nt(pick, c3))[:steps]


def _attn_sample(page_table, q, k_new, v_new, lf_new, cache_k, cache_v, cache_logf, *, pages=8):
    nb, steps, d = q.shape
    n_pages = page_table.shape[1]
    n_phys, page, n_heads = cache_logf.shape
    cache_k = jnp.transpose(cache_k, (0, 2, 3, 1)).reshape(n_phys, d, page)
    cache_v = jnp.transpose(cache_v, (0, 2, 3, 1)).reshape(n_phys, d, page)
    cache_logf = jnp.transpose(cache_logf, (0, 2, 1))
    pt_flat = page_table.reshape(-1)

    def page_map(r):
        return lambda i, c, pt: (pt[i * n_pages + c * pages + r], 0, 0)

    row = lambda i, c, pt: (i, 0, 0)
    kv_specs = [pl.BlockSpec((1, d, page), page_map(r)) for r in range(pages)]
    f_specs = [pl.BlockSpec((1, n_heads, page), page_map(r)) for r in range(pages)]
    grid_spec = pltpu.PrefetchScalarGridSpec(
        num_scalar_prefetch=1,
        grid=(nb, n_pages // pages),
        in_specs=[pl.BlockSpec((1, steps, d), row), pl.BlockSpec((1, steps, d), row),
                  pl.BlockSpec((1, steps, d), row), pl.BlockSpec((1, steps, n_heads), row)]
                 + kv_specs + kv_specs + f_specs,
        out_specs=pl.BlockSpec((1, steps, d), row),
        scratch_shapes=[pltpu.VMEM((LANES, d), BF16), pltpu.VMEM((d, LANES), F32),
                        pltpu.VMEM((LANES, 1), F32), pltpu.VMEM((LANES, 1), F32),
                        pltpu.VMEM((LANES, 1), F32), pltpu.VMEM((LANES, d), F32),
                        pltpu.VMEM((LANES, d), F32), pltpu.VMEM((LANES, LANES), F32)],
    )
    return pl.pallas_call(
        functools.partial(_sample_attn_kernel, pages=pages, steps=steps, d=d),
        grid_spec=grid_spec,
        out_shape=jax.ShapeDtypeStruct((nb, steps, d), F32),
        compiler_params=_params("arbitrary", "arbitrary"),
        name="attn_sample",
    )(pt_flat, q, k_new, v_new, lf_new, *([cache_k] * pages), *([cache_v] * pages), *([cache_logf] * pages))


def _outproj_kernel(o_ref, sz_ref, w_ref, postg_ref, gate_ref, x_ref, y_ref):
    g = (o_ref[0] * sz_ref[0].astype(F32)).astype(BF16)
    out = _dot(g, w_ref[...])
    y_ref[0] = x_ref[0] + gate_ref[0] * (_rms(out) * postg_ref[...])


def _outproj(o, sz, w, post_g, gate, x, *, tile, name):
    b, s, d = x.shape
    tok = pl.BlockSpec((1, tile, d), lambda i, j: (i, j, 0))
    per_row = gate.shape[1] == s
    mod = pl.BlockSpec((1, tile if per_row else 1, d),
                       (lambda i, j: (i, j, 0)) if per_row else (lambda i, j: (i, 0, 0)))
    return pl.pallas_call(
        _outproj_kernel,
        grid=(b, s // tile),
        in_specs=[tok, tok, _const_spec(w.shape), _const_spec((1, d)), mod, tok],
        out_specs=tok,
        out_shape=jax.ShapeDtypeStruct((b, s, d), F32),
        compiler_params=_params("arbitrary", "arbitrary"),
        name=name,
    )(o, sz, w, post_g, gate, x)


def kernel(x_prompt, x_sample, state_pool, cache_k, cache_v, cache_logf, page_table, c_prompt, c_sample,
           ada_w, ada_b, pre_g, post_g, a_in_w, a_grp_w, a_scale, a_out_w, kv_g, kv_w, f_b, b_in_w, b_out_w):
    bp, seq, d = x_prompt.shape
    bs, steps, _ = x_sample.shape
    n_heads = d // HEAD_DIM
    w_a = a_out_w.shape[1]
    rows_s = bs * steps

    pad = (-(bp + bs)) % 8
    c_all = jnp.concatenate([c_prompt, c_sample, jnp.zeros((pad, d), F32)], axis=0)
    mod = _adaln(c_all, ada_w, ada_b)

    def mods(layer):
        m = mod[layer]
        parts = [m[:, i * d:(i + 1) * d] for i in range(3)]
        prompt = [p[:bp].reshape(bp, 1, d) for p in parts]
        sample = [jnp.repeat(p[bp:bp + bs], steps, axis=0) for p in parts]
        return prompt, sample

    (shift0_p, scale0_p, gate0_p), (shift0_s, scale0_s, gate0_s) = mods(0)
    (shift1_p, scale1_p, gate1_p), (shift1_s, scale1_s, gate1_s) = mods(1)

    in_w = a_in_w[0].astype(BF16)
    grp_w = a_grp_w[0].astype(BF16)
    out_w = a_out_w[0].astype(BF16)
    kv_w_main = kv_w[:, :2 * d].astype(BF16)
    w_f = jnp.pad(kv_w[:, 2 * d:], ((0, 0), (0, LANES - n_heads))).astype(BF16)
    f_b_pad = jnp.pad(f_b, (0, LANES - n_heads)).reshape(1, LANES)
    bin_w = b_in_w[0].astype(BF16)
    bout_w = b_out_w[0].astype(BF16)
    pre0, pre1 = pre_g[0].reshape(1, d), pre_g[1].reshape(1, d)
    post0, post1 = post_g[0].reshape(1, d), post_g[1].reshape(1, d)
    kvg = kv_g.reshape(1, d)
    asc = a_scale[0].reshape(1, w_a)

    x1_p, tail_p = _pool_prompt(x_prompt, shift0_p, scale0_p, gate0_p, pre0, post0, in_w, grp_w, asc, out_w)
    pool_prompt = tail_p[None, :, HALO - POOL_BUF:, :]
    k_p, vt_p, sz_p, kb, vtb, qtb, fb16, lft_p = _proj(
        x1_p, shift1_p, scale1_p, kvg, pre1, kv_w_main, w_f, f_b_pad, bin_w, tile=512, prompt=True)
    logf_p = jnp.transpose(lft_p, (0, 2, 1))
    v_p = jnp.transpose(vt_p.reshape(bp, n_heads, HEAD_DIM, seq), (0, 3, 1, 2))
    o_p = _attn_prompt(qtb, kb, fb16, vtb, n_heads=n_heads)
    y_prompt = _outproj(o_p, sz_p, bout_w, post1, gate1_p, x1_p, tile=512, name="outproj_prompt")

    hist = jnp.transpose(state_pool[0], (1, 0, 2)).reshape(POOL_BUF * bs, w_a)
    x1_s, u_s = _pool_sample(x_sample.reshape(rows_s, d), shift0_s, scale0_s, gate0_s, pre0, post0,
                             in_w, hist, grp_w, asc, out_w, steps=steps)
    pool_sample = jnp.concatenate([state_pool[:, :, steps:], u_s.reshape(1, bs, steps, w_a)], axis=2)
    x1_s3 = x1_s.reshape(1, rows_s, d)
    k_s, v_s, sz_s, logf_s, q_s = _proj(
        x1_s3, shift1_s[None], scale1_s[None], kvg, pre1, kv_w_main, w_f, f_b_pad, bin_w,
        tile=rows_s, prompt=False)
    o_s = _attn_sample(page_table, q_s.reshape(bs, steps, d), k_s.reshape(bs, steps, d),
                       v_s.reshape(bs, steps, d), logf_s.reshape(bs, steps, n_heads),
                       cache_k, cache_v, cache_logf)
    y_sample = _outproj(o_s.reshape(1, rows_s, d), sz_s, bout_w, post1, gate1_s[None], x1_s3,
                        tile=rows_s, name="outproj_sample")

    return (y_prompt, y_sample.reshape(bs, steps, d), pool_prompt, pool_sample,
            k_p.reshape(bp, seq, n_heads, HEAD_DIM), v_p, logf_p,
            k_s.reshape(bs, steps, n_heads, HEAD_DIM), v_s.reshape(bs, steps, n_heads, HEAD_DIM),
            logf_s.reshape(bs, steps, n_heads))
```

```python
import functools

import jax
import jax.numpy as jnp
import numpy as np
from jax import lax
from jax.experimental import pallas as pl
from jax.experimental.pallas import tpu as pltpu

F32 = jnp.float32
BF16 = jnp.bfloat16

EPS = 1e-6
NEG = -1e30
LOG2E = 1.4426950408889634
POOL_WINDOWS = (2, 4, 8, 16)
POOL_BUF = max(POOL_WINDOWS) - 1
HEAD_DIM = 64
HALO = 16

V7X_VMEM_BYTES = 64 * 1024 * 1024
VMEM_LIMIT = V7X_VMEM_BYTES - 8 * 1024 * 1024
LANES = 128
BAND = 256
STRIP = 32


def _params(*sem):
    return pltpu.CompilerParams(dimension_semantics=sem, vmem_limit_bytes=VMEM_LIMIT)


def _const_spec(shape):
    nd = len(shape)
    return pl.BlockSpec(shape, lambda *_: (0,) * nd, pipeline_mode=pl.Buffered(1))


def _silu(x):
    return x * jax.nn.sigmoid(x)


def _rms(x):
    return x * lax.rsqrt(jnp.mean(x * x, axis=-1, keepdims=True) + EPS)


def _split3(x):
    a = x.astype(BF16)
    r = x - a.astype(F32)
    b = r.astype(BF16)
    c = (r - b.astype(F32)).astype(BF16)
    return a, b, c


def _log_sigmoid(x):
    return jnp.minimum(x, 0.0) - jnp.log1p(jnp.exp(-jnp.abs(x)))


def _div(x, n):
    return x >> (n.bit_length() - 1) if n & (n - 1) == 0 else x // n


def _mod(x, n):
    return x & (n - 1) if n & (n - 1) == 0 else x % n


def _dot(a, b):
    return jnp.dot(a, b, preferred_element_type=F32)


def _dot_nt(a, b):
    return lax.dot_general(a, b, (((1,), (1,)), ((), ())), preferred_element_type=F32)


def _dot_exact_rhs(sel, x):
    a, b, c = _split3(x)
    return _dot(sel, a) + _dot(sel, b) + _dot(sel, c)


def _dot_exact_lhs(x, sel):
    a, b, c = _split3(x)
    return _dot(a, sel) + _dot(b, sel) + _dot(c, sel)


def _adaln_kernel(c_ref, w_ref, b_ref, o_ref):
    a = _silu(c_ref[...]).astype(BF16)
    o_ref[0] = _dot(a, w_ref[0].astype(BF16)) + b_ref[0]


def _adaln(c_all, ada_w, ada_b):
    depth, d, d3 = ada_w.shape
    rows = c_all.shape[0]
    tn = d3 // 2
    return pl.pallas_call(
        _adaln_kernel,
        grid=(depth, d3 // tn),
        in_specs=[
            pl.BlockSpec((rows, d), lambda l, j: (0, 0)),
            pl.BlockSpec((1, d, tn), lambda l, j: (l, 0, j)),
            pl.BlockSpec((1, 1, tn), lambda l, j: (l, 0, j)),
        ],
        out_specs=pl.BlockSpec((1, rows, tn), lambda l, j: (l, 0, j)),
        out_shape=jax.ShapeDtypeStruct((depth, rows, d3), F32),
        compiler_params=_params("arbitrary", "arbitrary"),
        name="adaln",
    )(c_all, ada_w, ada_b.reshape(depth, 1, d3))


def _band_constants():
    t = np.arange(BAND)[:, None]
    s = np.arange(BAND)[None, :]
    eye = (t == s).astype(np.float32)

    def one(w, first):
        cnt = np.minimum(t + 1, w) if first else w
        return ((t - s >= 1) & (t - s < w)).astype(np.float32) - (cnt - 1) * eye

    band = np.stack([np.stack([one(w, first) for w in POOL_WINDOWS]) for first in (True, False)])
    th = np.arange(HALO)[:, None]
    jh = np.arange(HALO)[None, :]
    bandh = np.stack([(th - jh + HALO < w) for w in POOL_WINDOWS]).astype(np.float32)
    return jnp.asarray(band, BF16), jnp.asarray(bandh, BF16)


def _pool_prompt_kernel(x_ref, shift_ref, scale_ref, gate_ref, preg_ref, postg_ref, inw_ref,
                        band_ref, bandh_ref, grpw_ref, ascale_ref, outw_ref,
                        x1_ref, tail_ref, halo_ref, *, tile, w_a):
    t = pl.program_id(1)
    gw = w_a // len(POOL_WINDOWS)

    @pl.when(t == 0)
    def _():
        halo_ref[...] = jnp.zeros_like(halo_ref)

    x = x_ref[0]
    h = (_rms(x) * preg_ref[...]) * (1.0 + scale_ref[0]) + shift_ref[0]
    uz = _dot(h.astype(BF16), inw_ref[...])
    u = uz[:, :w_a]
    z = uz[:, w_a:]
    u_bf = u.astype(BF16)

    gated_blocks = []
    for blk in range(tile // BAND):
        r0 = blk * BAND
        halo = halo_ref[...] if blk == 0 else u_bf[r0 - HALO:r0]
        pos = t * tile + r0 + lax.broadcasted_iota(jnp.int32, (BAND, 1), 0)
        kind = jnp.where(t == 0, 0, 1) if blk == 0 else 1
        cols = []
        for g, w in enumerate(POOL_WINDOWS):
            c0 = g * gw
            ug = u_bf[r0:r0 + BAND, c0:c0 + gw]
            wsum = _dot(band_ref[kind, g], ug)
            top = wsum[:HALO] + _dot(bandh_ref[g], halo[:, c0:c0 + gw])
            wsum = jnp.concatenate([top, wsum[HALO:]], axis=0)
            cnt = jnp.minimum(pos + 1, w).astype(F32)
            pooled = wsum / cnt
            mixed = _dot(pooled.astype(BF16), grpw_ref[g]) * ascale_ref[:, c0:c0 + gw]
            cols.append((mixed * _silu(z[r0:r0 + BAND, c0:c0 + gw])).astype(BF16))
        gated_blocks.append(jnp.concatenate(cols, axis=1))
    gated = gated_blocks[0] if len(gated_blocks) == 1 else jnp.concatenate(gated_blocks, axis=0)

    y = _dot(gated, outw_ref[...])
    x1_ref[0] = x + gate_ref[0] * (_rms(y) * postg_ref[...])

    halo_ref[...] = u_bf[tile - HALO:]

    @pl.when(t == pl.num_programs(1) - 1)
    def _():
        tail_ref[0] = u[tile - HALO:]


def _pool_prompt(x, shift, scale, gate, pre_g, post_g, in_w, grp_w, a_scale, out_w, *, tile=256):
    b, s, d = x.shape
    w_a = out_w.shape[0]
    band, bandh = _band_constants()
    tok = pl.BlockSpec((1, tile, d), lambda i, j: (i, j, 0))
    mod = pl.BlockSpec((1, 1, d), lambda i, j: (i, 0, 0))
    return pl.pallas_call(
        functools.partial(_pool_prompt_kernel, tile=tile, w_a=w_a),
        grid=(b, s // tile),
        in_specs=[tok, mod, mod, mod, _const_spec((1, d)), _const_spec((1, d)),
                  _const_spec(in_w.shape), _const_spec(band.shape), _const_spec(bandh.shape),
                  _const_spec(grp_w.shape), _const_spec((1, w_a)), _const_spec(out_w.shape)],
        out_specs=[tok, pl.BlockSpec((1, HALO, w_a), lambda i, j: (i, 0, 0))],
        out_shape=[jax.ShapeDtypeStruct((b, s, d), F32),
                   jax.ShapeDtypeStruct((b, HALO, w_a), F32)],
        scratch_shapes=[pltpu.VMEM((HALO, w_a), BF16)],
        compiler_params=_params("arbitrary", "arbitrary"),
        name="pool_prompt",
    )(x, shift, scale, gate, pre_g, post_g, in_w, band, bandh, grp_w, a_scale, out_w)


def _pool_sample_kernel(x_ref, shift_ref, scale_ref, gate_ref, preg_ref, postg_ref, inw_ref,
                        hist_ref, grpw_ref, ascale_ref, outw_ref, x1_ref, u_ref, *, steps, w_a):
    gw = w_a // len(POOL_WINDOWS)
    rows = x_ref.shape[0]
    nh = hist_ref.shape[0]

    x = x_ref[...]
    h = (_rms(x) * preg_ref[...]) * (1.0 + scale_ref[...]) + shift_ref[...]
    uz = _dot(h.astype(BF16), inw_ref[...])
    u = uz[:, :w_a]
    z = uz[:, w_a:]
    u_ref[...] = u
    u_bf = u.astype(BF16)
    hist_bf = hist_ref[...].astype(BF16)

    n_batch = nh // POOL_BUF
    ro = lax.broadcasted_iota(jnp.int32, (rows, rows), 0)
    ri = lax.broadcasted_iota(jnp.int32, (rows, rows), 1)
    same = _div(ro, steps) == _div(ri, steps)
    lag_new = _mod(ro, steps) - _mod(ri, steps)
    roh = lax.broadcasted_iota(jnp.int32, (rows, nh), 0)
    ch = lax.broadcasted_iota(jnp.int32, (rows, nh), 1)
    in_batch = _mod(ch, n_batch) == _div(roh, steps)
    lag_hist = POOL_BUF + _mod(roh, steps) - _div(ch, n_batch)

    cols = []
    for g, w in enumerate(POOL_WINDOWS):
        c0 = g * gw
        sel_new = jnp.where(same & (lag_new >= 1) & (lag_new < w), 1.0, 0.0)
        sel_new = jnp.where(ro == ri, -(w - 1.0), sel_new).astype(BF16)
        sel_hist = jnp.where(in_batch & (lag_hist < w), 1.0, 0.0).astype(BF16)
        wsum = _dot(sel_new, u_bf[:, c0:c0 + gw]) + _dot(sel_hist, hist_bf[:, c0:c0 + gw])
        pooled = wsum / float(w)
        mixed = _dot(pooled.astype(BF16), grpw_ref[g]) * ascale_ref[:, c0:c0 + gw]
        cols.append((mixed * _silu(z[:, c0:c0 + gw])).astype(BF16))
    gated = jnp.concatenate(cols, axis=1)
    y = _dot(gated, outw_ref[...])
    x1_ref[...] = x + gate_ref[...] * (_rms(y) * postg_ref[...])


def _pool_sample(x, shift, scale, gate, pre_g, post_g, in_w, hist, grp_w, a_scale, out_w, *, steps):
    rows, d = x.shape
    w_a = out_w.shape[0]
    args = (x, shift, scale, gate, pre_g, post_g, in_w, hist, grp_w, a_scale, out_w)
    return pl.pallas_call(
        functools.partial(_pool_sample_kernel, steps=steps, w_a=w_a),
        grid=(1,),
        in_specs=[_const_spec(a.shape) for a in args],
        out_specs=[pl.BlockSpec((rows, d), lambda i: (0, 0)),
                   pl.BlockSpec((rows, w_a), lambda i: (0, 0))],
        out_shape=[jax.ShapeDtypeStruct((rows, d), F32),
                   jax.ShapeDtypeStruct((rows, w_a), F32)],
        compiler_params=_params("arbitrary"),
        name="pool_sample",
    )(*args)


def _proj_kernel(x_ref, shift_ref, scale_ref, kvg_ref, preg_ref, kvw_ref, wf_ref, fb_ref, binw_ref,
                 *rest, tile, d, prompt):
    if prompt:
        ltri_ref, place_ref, k_ref, vt_ref, sz_ref, kb_ref, vtb_ref, qtb_ref, fb16_ref, lft_ref, carry_ref = rest
    else:
        k_ref, v_ref, sz_ref, logf_ref, q_ref = rest
    n_heads = d // HEAD_DIM

    r = _rms(x_ref[0])
    xn = (r * kvg_ref[...]).astype(BF16)
    h = ((r * preg_ref[...]) * (1.0 + scale_ref[0]) + shift_ref[0]).astype(BF16)

    kv = _dot(xn, kvw_ref[...])
    k = kv[:, :d]
    v = kv[:, d:]
    k_ref[0] = k
    logf = _log_sigmoid(_dot(xn, wf_ref[...]) + fb_ref[...])

    qz = _dot(h, binw_ref[...])
    sz_ref[0] = _silu(qz[:, d:]).astype(BF16)

    if prompt:
        qt = (qz[:, :d] * (HEAD_DIM ** -0.5 * LOG2E)).T
        vt = v.T
        vt_ref[0] = vt
        for hp in range(d // LANES):
            sl = slice(hp * LANES, (hp + 1) * LANES)
            kb_ref[0, hp] = k[:, sl].astype(BF16)
            vtb_ref[0, hp] = vt[sl].astype(BF16)
            qtb_ref[0, hp] = qt[sl].astype(BF16)

        @pl.when(pl.program_id(1) == 0)
        def _():
            carry_ref[...] = jnp.zeros_like(carry_ref)

        lft_ref[0] = logf.T[:n_heads]
        cum = _dot_exact_rhs(ltri_ref[...], logf) + carry_ref[...]
        carry_ref[...] = cum[tile - 1:tile]
        hi, mid, lo = _split3(cum * LOG2E)
        fb16_ref[0] = (_dot(hi, place_ref[0]) + _dot(mid, place_ref[1]) + _dot(lo, place_ref[2])).astype(BF16)
    else:
        v_ref[0] = v
        logf_ref[0] = logf[:, :n_heads]
        q_ref[0] = qz[:, :d] * (HEAD_DIM ** -0.5)


def _proj(x, shift, scale, kv_g, pre_g, kv_w_main, w_f, f_b, b_in_w, *, tile, prompt):
    b, s, d = x.shape
    n_heads = d // HEAD_DIM
    hp = d // LANES
    tok = pl.BlockSpec((1, tile, d), lambda i, j: (i, j, 0))
    mod_rows = shift.shape[1]
    mod_tile = tile if mod_rows == s else 1
    mod = pl.BlockSpec((1, mod_tile, d), (lambda i, j: (i, j, 0)) if mod_rows == s else (lambda i, j: (i, 0, 0)))
    in_specs = [tok, mod, mod, _const_spec((1, d)), _const_spec((1, d)), _const_spec(kv_w_main.shape),
                _const_spec(w_f.shape), _const_spec(f_b.shape), _const_spec(b_in_w.shape)]
    args = [x, shift, scale, kv_g, pre_g, kv_w_main, w_f, f_b, b_in_w]
    scratch = []
    if prompt:
        ltri = jnp.asarray(np.tril(np.ones((tile, tile), np.float32)), BF16)
        place = np.zeros((3, LANES, LANES), np.float32)
        for term in range(3):
            place[term, np.arange(n_heads), term * n_heads + np.arange(n_heads)] = 1.0
        place = jnp.asarray(place, BF16)
        in_specs += [_const_spec(ltri.shape), _const_spec(place.shape)]
        args += [ltri, place]
        pair = pl.BlockSpec((1, hp, tile, LANES), lambda i, j: (i, 0, j, 0))
        pair_t = pl.BlockSpec((1, hp, LANES, tile), lambda i, j: (i, 0, 0, j))
        out_specs = [tok, pl.BlockSpec((1, d, tile), lambda i, j: (i, 0, j)), tok, pair, pair_t, pair_t,
                     pl.BlockSpec((1, tile, LANES), lambda i, j: (i, j, 0)),
                     pl.BlockSpec((1, n_heads, tile), lambda i, j: (i, 0, j))]
        out_shape = [jax.ShapeDtypeStruct((b, s, d), F32), jax.ShapeDtypeStruct((b, d, s), F32),
                     jax.ShapeDtypeStruct((b, s, d), BF16), jax.ShapeDtypeStruct((b, hp, s, LANES), BF16),
                     jax.ShapeDtypeStruct((b, hp, LANES, s), BF16), jax.ShapeDtypeStruct((b, hp, LANES, s), BF16),
                     jax.ShapeDtypeStruct((b, s, LANES), BF16), jax.ShapeDtypeStruct((b, n_heads, s), F32)]
        scratch = [pltpu.VMEM((1, LANES), F32)]
    else:
        out_specs = [tok, tok, tok, pl.BlockSpec((1, tile, n_heads), lambda i, j: (i, j, 0)), tok]
        out_shape = [jax.ShapeDtypeStruct((b, s, d), F32), jax.ShapeDtypeStruct((b, s, d), F32),
                     jax.ShapeDtypeStruct((b, s, d), BF16), jax.ShapeDtypeStruct((b, s, n_heads), F32),
                     jax.ShapeDtypeStruct((b, s, d), F32)]
    return pl.pallas_call(
        functools.partial(_proj_kernel, tile=tile, d=d, prompt=prompt),
        grid=(b, s // tile),
        in_specs=in_specs,
        out_specs=out_specs,
        out_shape=out_shape,
        scratch_shapes=scratch,
        compiler_params=_params("arbitrary", "arbitrary"),
        name="proj_prompt" if prompt else "proj_sample",
    )(*args)


def _prompt_tile(qt_ref, k_ref, f_ref, vt_ref, o_ref, sa_ref, sb_ref, pa_ref, pb_ref, *, tq, n_heads):
    tk = tq
    pair = pl.program_id(1)
    qi = pl.program_id(2)
    qt = qt_ref[0, 0]
    row = lax.broadcasted_iota(jnp.int32, (LANES, 1), 0)
    zero = jnp.zeros_like(qt)
    qat = []
    for hd in range(2):
        head = 2 * pair + hd
        mine = (row >= hd * HEAD_DIM) & (row < (hd + 1) * HEAD_DIM)
        pick = (row == head) | (row == n_heads + head) | (row == 2 * n_heads + head)
        minus = jnp.where(pick, -1.0, 0.0).astype(BF16)
        qat.append(jnp.concatenate([jnp.where(mine, qt, zero), jnp.broadcast_to(minus, (LANES, tq))], axis=0))

    def qk(tile, s_ref):
        k0 = pl.multiple_of(tile * tk, tk)
        ka = jnp.concatenate([k_ref[0, 0, pl.ds(k0, tk), :], f_ref[0, pl.ds(k0, tk), :]], axis=1)
        for hd in range(2):
            s_ref[hd] = _dot(ka, qat[hd])

    ones = jnp.ones((8, tk), BF16)

    def pv(tile, p_ref):
        k0 = pl.multiple_of(tile * tk, tk)
        vt = vt_ref[0, 0, :, pl.ds(k0, tk)]
        return [_dot(jnp.concatenate([vt[hd * HEAD_DIM:(hd + 1) * HEAD_DIM], ones], axis=0), p_ref[hd])
                for hd in range(2)]

    def softmax(s_ref, p_ref, stats, diagonal=False):
        new_stats, alphas = [], []

        def scores(hd, r0):
            st = s_ref[hd, r0:r0 + STRIP, :]
            if diagonal:
                key = lax.broadcasted_iota(jnp.int32, (STRIP, tq), 0) + r0
                qry = lax.broadcasted_iota(jnp.int32, (STRIP, tq), 1)
                st = jnp.where(key <= qry, st, NEG)
            return st

        def fold(x, op):
            return op(x.reshape(STRIP // 8, 8, tq), axis=0)

        for hd in range(2):
            m = stats[hd]
            top = jnp.full((8, tq), NEG, F32)
            for r0 in range(0, tk, STRIP):
                top = jnp.maximum(top, fold(scores(hd, r0), jnp.max))
            m_new = jnp.maximum(m, jnp.max(top, axis=0, keepdims=True))
            alphas.append(jnp.exp2(m - m_new))
            for r0 in range(0, tk, STRIP):
                p_ref[hd, r0:r0 + STRIP, :] = jnp.exp2((scores(hd, r0) - m_new).astype(BF16))
            new_stats.append(m_new)
        return new_stats, alphas

    def rescale_add(alpha, acc, part):
        return [alpha[hd] * acc[hd] + part[hd] for hd in range(2)]

    qk(0, sa_ref)
    pb_ref[...] = jnp.zeros_like(pb_ref)
    stats = [jnp.full((1, tq), NEG, F32) for _ in range(2)]
    alpha_b = [jnp.ones((1, tq), F32) for _ in range(2)]
    acc = [jnp.zeros((HEAD_DIM + 8, tq), F32) for _ in range(2)]

    def body(t, carry):
        stats, alpha_b, acc = carry
        a = 2 * t
        part = pv(jnp.maximum(a - 1, 0), pb_ref)
        qk(a + 1, sb_ref)
        stats, alpha_a = softmax(sa_ref, pa_ref, stats)
        acc = rescale_add(alpha_b, acc, part)
        part = pv(a, pa_ref)
        qk(a + 2, sa_ref)
        stats, alpha_b = softmax(sb_ref, pb_ref, stats)
        acc = rescale_add(alpha_a, acc, part)
        return stats, alpha_b, acc

    pairs = qi // 2
    stats, alpha_b, acc = lax.fori_loop(0, pairs, body, (stats, alpha_b, acc))
    a = 2 * pairs

    def finish(acc):
        o_ref[0] = jnp.concatenate(
            [acc[hd][:HEAD_DIM] / acc[hd][HEAD_DIM:HEAD_DIM + 1] for hd in range(2)], axis=0).T

    @pl.when(a == qi)
    def _():
        part = pv(jnp.maximum(a - 1, 0), pb_ref)
        _, alpha_a = softmax(sa_ref, pa_ref, stats, diagonal=True)
        out = rescale_add(alpha_b, acc, part)
        finish(rescale_add(alpha_a, out, pv(a, pa_ref)))

    @pl.when(a != qi)
    def _():
        part = pv(jnp.maximum(a - 1, 0), pb_ref)
        qk(a + 1, sb_ref)
        mid, alpha_a = softmax(sa_ref, pa_ref, stats)
        out = rescale_add(alpha_b, acc, part)
        part = pv(a, pa_ref)
        _, alpha_d = softmax(sb_ref, pb_ref, mid, diagonal=True)
        out = rescale_add(alpha_a, out, part)
        finish(rescale_add(alpha_d, out, pv(a + 1, pb_ref)))


def _forget_prefix(blocks, carry):
    n_heads, page = blocks[0].shape
    stacked = jnp.concatenate(blocks, axis=0)
    r = lax.broadcasted_iota(jnp.int32, (page, page), 0)
    c = lax.broadcasted_iota(jnp.int32, (page, page), 1)
    within = _dot_exact_lhs(stacked, jnp.where(r <= c, 1.0, 0.0).astype(BF16))
    sums = []
    for b in range(len(blocks)):
        blk = within[b * n_heads:(b + 1) * n_heads] + carry
        sums.append(blk)
        carry = blk[:, page - 1:page]
    return sums, carry


def _sample_chunk(chunk, last, very_first, q_ref, knew_ref, vnew_ref, lfnew_ref, k_pages, v_pages, f_pages,
                  o_ref, qbd_ref, acc_ref, m_ref, l_ref, fc_ref, knew_buf, vnew_buf, lfnew_buf, *, steps, d):
    n_heads = d // HEAD_DIM
    page = k_pages[0].shape[2]
    rows = n_heads * steps
    assert page == LANES and rows % 8 == 0 and 3 * n_heads <= LANES

    row_id = lax.broadcasted_iota(jnp.int32, (rows, 1), 0)
    row_head = _div(row_id, steps)
    row_step = _mod(row_id, steps)
    lane_head = _div(lax.broadcasted_iota(jnp.int32, (1, d), 1), HEAD_DIM)

    @pl.when(very_first)
    def _():
        knew_buf[...] = jnp.zeros_like(knew_buf)
        vnew_buf[...] = jnp.zeros_like(vnew_buf)
        lfnew_buf[...] = jnp.zeros_like(lfnew_buf)

    @pl.when(chunk == 0)
    def _():
        q = q_ref[0]
        qbd = jnp.zeros((rows, d), F32)
        for i in range(steps):
            qbd = jnp.where((lane_head == row_head) & (row_step == i), q[i:i + 1, :], qbd)
        lane = lax.broadcasted_iota(jnp.int32, (1, LANES), 1)
        pick = (lane == row_head) | (lane == n_heads + row_head) | (lane == 2 * n_heads + row_head)
        qbd_ref[...] = jnp.concatenate([qbd, jnp.where(pick, -1.0, 0.0)], axis=1)
        m_ref[...] = jnp.full_like(m_ref, NEG)
        l_ref[...] = jnp.zeros_like(l_ref)
        acc_ref[...] = jnp.zeros_like(acc_ref)
        fc_ref[...] = jnp.zeros_like(fc_ref)

    def online_update(st, value_dot):
        m = m_ref[...]
        m_new = jnp.maximum(m, jnp.max(st, axis=1, keepdims=True))
        alpha = jnp.exp(m - m_new)
        p = jnp.exp(st - m_new)
        l_ref[...] = alpha * l_ref[...] + jnp.sum(p, axis=1, keepdims=True)
        m_ref[...] = m_new
        acc_ref[...] = alpha * acc_ref[...] + value_dot(p)

    sums, carry = _forget_prefix([r[0] for r in f_pages], fc_ref[...])
    fc_ref[...] = carry
    hi, mid, lo = _split3(jnp.concatenate(sums, axis=1))
    keys = hi.shape[1]
    terms = jnp.concatenate([hi.astype(F32), mid.astype(F32), lo.astype(F32),
                             jnp.zeros((LANES - 3 * n_heads, keys), F32)], axis=0)
    kt = jnp.concatenate([jnp.concatenate([r[0] for r in k_pages], axis=1), terms], axis=0)
    vt = jnp.concatenate([r[0] for r in v_pages], axis=1)
    online_update(_dot(qbd_ref[...], kt), lambda p: _dot_nt(p, vt))

    @pl.when(chunk == last)
    def _():
        knew_buf[0:steps, :] = knew_ref[0]
        vnew_buf[0:steps, :] = vnew_ref[0]
        lfnew_buf[0:steps, 0:n_heads] = lfnew_ref[0]
        new_sums, _ = _forget_prefix([lfnew_buf[...].T[:n_heads]], fc_ref[...])
        expand = jnp.where(lax.broadcasted_iota(jnp.int32, (rows, n_heads), 1) == row_head, 1.0, 0.0).astype(BF16)
        st = _dot_nt(qbd_ref[:, :d], knew_buf[...]) - _dot_exact_rhs(expand, new_sums[0])
        key = lax.broadcasted_iota(jnp.int32, (1, LANES), 1)
        online_update(jnp.where(key <= row_step, st, NEG), lambda p: _dot(p, vnew_buf[...]))

        res = jnp.where(lane_head == row_head, acc_ref[...] / l_ref[...], 0.0)
        gather = jnp.where(lax.broadcasted_iota(jnp.int32, (8, rows), 0) ==
                           _mod(lax.broadcasted_iota(jnp.int32, (8, rows), 1), steps), 1.0, 0.0).astype(BF16)
        o_ref[0] = _dot_exact_rhs(gather, res)[:steps]


def _attn_kernel(pt_ref, qt_ref, k_ref, f_ref, vt_ref, qs_ref, knew_ref, vnew_ref, lfnew_ref, *rest,
                 tq, n_heads, pages, steps, d):
    del pt_ref
    k_pages, v_pages, f_pages = rest[:pages], rest[pages:2 * pages], rest[2 * pages:3 * pages]
    (o_ref, os_ref, sa_ref, sb_ref, pa_ref, pb_ref,
     qbd_ref, acc_ref, m_ref, l_ref, fc_ref, knew_buf, vnew_buf, lfnew_buf) = rest[3 * pages:]
    chunk = pl.program_id(2)
    very_first = (pl.program_id(0) == 0) & (pl.program_id(1) == 0) & (chunk == 0)
    _sample_chunk(chunk, pl.num_programs(2) - 1, very_first, qs_ref, knew_ref, vnew_ref, lfnew_ref,
                  k_pages, v_pages, f_pages, os_ref, qbd_ref, acc_ref, m_ref, l_ref, fc_ref,
                  knew_buf, vnew_buf, lfnew_buf, steps=steps, d=d)
    _prompt_tile(qt_ref, k_ref, f_ref, vt_ref, o_ref, sa_ref, sb_ref, pa_ref, pb_ref, tq=tq, n_heads=n_heads)


def _attention(qtb, kb, fb16, vtb, page_table, q_s, k_new, v_new, lf_new, cache_k, cache_v, cache_logf,
               *, n_heads, tq=512):
    b, hp, s, _ = kb.shape
    nb, steps, d = q_s.shape
    n_pages = page_table.shape[1]
    n_phys, page, _ = cache_logf.shape
    rows = n_heads * steps
    n_tiles = s // tq
    assert nb == b * hp and n_pages % n_tiles == 0
    pages = n_pages // n_tiles
    cache_k = jnp.transpose(cache_k, (0, 2, 3, 1)).reshape(n_phys, d, page)
    cache_v = jnp.transpose(cache_v, (0, 2, 3, 1)).reshape(n_phys, d, page)
    cache_logf = jnp.transpose(cache_logf, (0, 2, 1))

    def page_map(r):
        return lambda i, p, j, pt: (pt[(i * hp + p) * n_pages + j * pages + r], 0, 0)

    batch = lambda i, p, j, pt: (i * hp + p, 0, 0)
    kv_specs = [pl.BlockSpec((1, d, page), page_map(r)) for r in range(pages)]
    f_specs = [pl.BlockSpec((1, n_heads, page), page_map(r)) for r in range(pages)]
    grid_spec = pltpu.PrefetchScalarGridSpec(
        num_scalar_prefetch=1,
        grid=(b, hp, n_tiles),
        in_specs=[pl.BlockSpec((1, 1, LANES, tq), lambda i, p, j, pt: (i, p, 0, j)),
                  pl.BlockSpec((1, 1, s, LANES), lambda i, p, j, pt: (i, p, 0, 0)),
                  pl.BlockSpec((1, s, LANES), lambda i, p, j, pt: (i, 0, 0)),
                  pl.BlockSpec((1, 1, LANES, s), lambda i, p, j, pt: (i, p, 0, 0)),
                  pl.BlockSpec((1, steps, d), batch), pl.BlockSpec((1, steps, d), batch),
                  pl.BlockSpec((1, steps, d), batch), pl.BlockSpec((1, steps, n_heads), batch)]
                 + kv_specs + kv_specs + f_specs,
        out_specs=[pl.BlockSpec((1, tq, LANES), lambda i, p, j, pt: (i, j, p)),
                   pl.BlockSpec((1, steps, d), batch)],
        scratch_shapes=[pltpu.VMEM((2, tq, tq), F32), pltpu.VMEM((2, tq, tq), F32),
                        pltpu.VMEM((2, tq, tq), BF16), pltpu.VMEM((2, tq, tq), BF16),
                        pltpu.VMEM((rows, d + LANES), F32), pltpu.VMEM((rows, d), F32),
                        pltpu.VMEM((rows, 1), F32), pltpu.VMEM((rows, 1), F32),
                        pltpu.VMEM((n_heads, 1), F32), pltpu.VMEM((LANES, d), F32),
                        pltpu.VMEM((LANES, d), F32), pltpu.VMEM((LANES, LANES), F32)],
    )
    return pl.pallas_call(
        functools.partial(_attn_kernel, tq=tq, n_heads=n_heads, pages=pages, steps=steps, d=d),
        grid_spec=grid_spec,
        out_shape=[jax.ShapeDtypeStruct((b, s, hp * LANES), F32), jax.ShapeDtypeStruct((nb, steps, d), F32)],
        compiler_params=_params("arbitrary", "arbitrary", "arbitrary"),
        name="attention",
    )(page_table.reshape(-1), qtb, kb, fb16, vtb, q_s, k_new, v_new, lf_new,
      *([cache_k] * pages), *([cache_v] * pages), *([cache_logf] * pages))


def _outproj_kernel(o_ref, sz_ref, w_ref, postg_ref, gate_ref, x_ref, y_ref):
    g = (o_ref[0] * sz_ref[0].astype(F32)).astype(BF16)
    out = _dot(g, w_ref[...])
    y_ref[0] = x_ref[0] + gate_ref[0] * (_rms(out) * postg_ref[...])


def _outproj(o, sz, w, post_g, gate, x, *, tile, name):
    b, s, d = x.shape
    tok = pl.BlockSpec((1, tile, d), lambda i, j: (i, j, 0))
    per_row = gate.shape[1] == s
    mod = pl.BlockSpec((1, tile if per_row else 1, d),
                       (lambda i, j: (i, j, 0)) if per_row else (lambda i, j: (i, 0, 0)))
    return pl.pallas_call(
        _outproj_kernel,
        grid=(b, s // tile),
        in_specs=[tok, tok, _const_spec(w.shape), _const_spec((1, d)), mod, tok],
        out_specs=tok,
        out_shape=jax.ShapeDtypeStruct((b, s, d), F32),
        compiler_params=_params("arbitrary", "arbitrary"),
        name=name,
    )(o, sz, w, post_g, gate, x)


def kernel(x_prompt, x_sample, state_pool, cache_k, cache_v, cache_logf, page_table, c_prompt, c_sample,
           ada_w, ada_b, pre_g, post_g, a_in_w, a_grp_w, a_scale, a_out_w, kv_g, kv_w, f_b, b_in_w, b_out_w):
    bp, seq, d = x_prompt.shape
    bs, steps, _ = x_sample.shape
    n_heads = d // HEAD_DIM
    w_a = a_out_w.shape[1]
    rows_s = bs * steps

    pad = (-(bp + bs)) % 8
    c_all = jnp.concatenate([c_prompt, c_sample, jnp.zeros((pad, d), F32)], axis=0)
    mod = _adaln(c_all, ada_w, ada_b)

    def mods(layer):
        m = mod[layer]
        parts = [m[:, i * d:(i + 1) * d] for i in range(3)]
        prompt = [p[:bp].reshape(bp, 1, d) for p in parts]
        sample = [jnp.repeat(p[bp:bp + bs], steps, axis=0) for p in parts]
        return prompt, sample

    (shift0_p, scale0_p, gate0_p), (shift0_s, scale0_s, gate0_s) = mods(0)
    (shift1_p, scale1_p, gate1_p), (shift1_s, scale1_s, gate1_s) = mods(1)

    in_w = a_in_w[0].astype(BF16)
    grp_w = a_grp_w[0].astype(BF16)
    out_w = a_out_w[0].astype(BF16)
    kv_w_main = kv_w[:, :2 * d].astype(BF16)
    w_f = jnp.pad(kv_w[:, 2 * d:], ((0, 0), (0, LANES - n_heads))).astype(BF16)
    f_b_pad = jnp.pad(f_b, (0, LANES - n_heads)).reshape(1, LANES)
    bin_w = b_in_w[0].astype(BF16)
    bout_w = b_out_w[0].astype(BF16)
    pre0, pre1 = pre_g[0].reshape(1, d), pre_g[1].reshape(1, d)
    post0, post1 = post_g[0].reshape(1, d), post_g[1].reshape(1, d)
    kvg = kv_g.reshape(1, d)
    asc = a_scale[0].reshape(1, w_a)

    hist = jnp.transpose(state_pool[0], (1, 0, 2)).reshape(POOL_BUF * bs, w_a)
    x1_s, u_s = _pool_sample(x_sample.reshape(rows_s, d), shift0_s, scale0_s, gate0_s, pre0, post0,
                             in_w, hist, grp_w, asc, out_w, steps=steps)
    pool_sample = jnp.concatenate([state_pool[:, :, steps:], u_s.reshape(1, bs, steps, w_a)], axis=2)
    x1_s3 = x1_s.reshape(1, rows_s, d)
    k_s, v_s, sz_s, logf_s, q_s = _proj(
        x1_s3, shift1_s[None], scale1_s[None], kvg, pre1, kv_w_main, w_f, f_b_pad, bin_w,
        tile=rows_s, prompt=False)

    x1_p, tail_p = _pool_prompt(x_prompt, shift0_p, scale0_p, gate0_p, pre0, post0, in_w, grp_w, asc, out_w)
    pool_prompt = tail_p[None, :, HALO - POOL_BUF:, :]
    k_p, vt_p, sz_p, kb, vtb, qtb, fb16, lft_p = _proj(
        x1_p, shift1_p, scale1_p, kvg, pre1, kv_w_main, w_f, f_b_pad, bin_w, tile=512, prompt=True)
    logf_p = jnp.transpose(lft_p, (0, 2, 1))
    v_p = jnp.transpose(vt_p.reshape(bp, n_heads, HEAD_DIM, seq), (0, 3, 1, 2))

    o_p, o_s = _attention(qtb, kb, fb16, vtb, page_table, q_s.reshape(bs, steps, d), k_s.reshape(bs, steps, d),
                          v_s.reshape(bs, steps, d), logf_s.reshape(bs, steps, n_heads),
                          cache_k, cache_v, cache_logf, n_heads=n_heads)
    y_prompt = _outproj(o_p, sz_p, bout_w, post1, gate1_p, x1_p, tile=512, name="outproj_prompt")
    y_sample = _outproj(o_s.reshape(1, rows_s, d), sz_s, bout_w, post1, gate1_s[None], x1_s3,
                        tile=rows_s, name="outproj_sample")

    return (y_prompt, y_sample.reshape(bs, steps, d), pool_prompt, pool_sample,
            k_p.reshape(bp, seq, n_heads, HEAD_DIM), v_p, logf_p,
            k_s.reshape(bs, steps, n_heads, HEAD_DIM), v_s.reshape(bs, steps, n_heads, HEAD_DIM),
            logf_s.reshape(bs, steps, n_heads))
```

```python
import functools

import jax
import jax.numpy as jnp
import numpy as np
from jax import lax
from jax.experimental import pallas as pl
from jax.experimental.pallas import tpu as pltpu

F32 = jnp.float32
BF16 = jnp.bfloat16

EPS = 1e-6
NEG = -1e30
LOG2E = 1.4426950408889634
POOL_WINDOWS = (2, 4, 8, 16)
POOL_BUF = max(POOL_WINDOWS) - 1
HEAD_DIM = 64
HALO = 16

V7X_VMEM_BYTES = 64 * 1024 * 1024
VMEM_LIMIT = V7X_VMEM_BYTES - 8 * 1024 * 1024
LANES = 128
BAND = 256
STRIP = 32


def _params(*sem):
    return pltpu.CompilerParams(dimension_semantics=sem, vmem_limit_bytes=VMEM_LIMIT)


def _const_spec(shape):
    nd = len(shape)
    return pl.BlockSpec(shape, lambda *_: (0,) * nd, pipeline_mode=pl.Buffered(1))


def _silu(x):
    return x * jax.nn.sigmoid(x)


def _rms(x):
    return x * lax.rsqrt(jnp.mean(x * x, axis=-1, keepdims=True) + EPS)


def _split3(x):
    a = x.astype(BF16)
    r = x - a.astype(F32)
    b = r.astype(BF16)
    c = (r - b.astype(F32)).astype(BF16)
    return a, b, c


def _log_sigmoid(x):
    return jnp.minimum(x, 0.0) - jnp.log1p(jnp.exp(-jnp.abs(x)))


def _div(x, n):
    return x >> (n.bit_length() - 1) if n & (n - 1) == 0 else x // n


def _mod(x, n):
    return x & (n - 1) if n & (n - 1) == 0 else x % n


def _dot(a, b):
    return jnp.dot(a, b, preferred_element_type=F32)


def _dot_nt(a, b):
    return lax.dot_general(a, b, (((1,), (1,)), ((), ())), preferred_element_type=F32)


def _dot_exact_rhs(sel, x):
    a, b, c = _split3(x)
    return _dot(sel, a) + _dot(sel, b) + _dot(sel, c)


def _dot_exact_lhs(x, sel):
    a, b, c = _split3(x)
    return _dot(a, sel) + _dot(b, sel) + _dot(c, sel)


def _adaln_kernel(c_ref, w_ref, b_ref, o_ref):
    a = _silu(c_ref[...]).astype(BF16)
    o_ref[0] = _dot(a, w_ref[0].astype(BF16)) + b_ref[0]


def _adaln(c_all, ada_w, ada_b):
    depth, d, d3 = ada_w.shape
    rows = c_all.shape[0]
    tn = d3 // 2
    return pl.pallas_call(
        _adaln_kernel,
        grid=(depth, d3 // tn),
        in_specs=[
            pl.BlockSpec((rows, d), lambda l, j: (0, 0)),
            pl.BlockSpec((1, d, tn), lambda l, j: (l, 0, j)),
            pl.BlockSpec((1, 1, tn), lambda l, j: (l, 0, j)),
        ],
        out_specs=pl.BlockSpec((1, rows, tn), lambda l, j: (l, 0, j)),
        out_shape=jax.ShapeDtypeStruct((depth, rows, d3), F32),
        compiler_params=_params("arbitrary", "arbitrary"),
        name="adaln",
    )(c_all, ada_w, ada_b.reshape(depth, 1, d3))


def _band_constants():
    t = np.arange(BAND)[:, None]
    s = np.arange(BAND)[None, :]
    eye = (t == s).astype(np.float32)

    def one(w, first):
        cnt = np.minimum(t + 1, w) if first else w
        return ((t - s >= 1) & (t - s < w)).astype(np.float32) - (cnt - 1) * eye

    band = np.stack([np.stack([one(w, first) for w in POOL_WINDOWS]) for first in (True, False)])
    th = np.arange(HALO)[:, None]
    jh = np.arange(HALO)[None, :]
    bandh = np.stack([(th - jh + HALO < w) for w in POOL_WINDOWS]).astype(np.float32)
    return jnp.asarray(band, BF16), jnp.asarray(bandh, BF16)


def _pool_prompt_kernel(x_ref, shift_ref, scale_ref, gate_ref, preg_ref, postg_ref, inw_ref,
                        band_ref, bandh_ref, grpw_ref, ascale_ref, outw_ref,
                        x1_ref, tail_ref, halo_ref, *, tile, w_a):
    t = pl.program_id(1)
    gw = w_a // len(POOL_WINDOWS)

    @pl.when(t == 0)
    def _():
        halo_ref[...] = jnp.zeros_like(halo_ref)

    x = x_ref[0]
    h = (_rms(x) * preg_ref[...]) * (1.0 + scale_ref[0]) + shift_ref[0]
    uz = _dot(h.astype(BF16), inw_ref[...])
    u = uz[:, :w_a]
    z = uz[:, w_a:]
    u_bf = u.astype(BF16)

    gated_blocks = []
    for blk in range(tile // BAND):
        r0 = blk * BAND
        halo = halo_ref[...] if blk == 0 else u_bf[r0 - HALO:r0]
        pos = t * tile + r0 + lax.broadcasted_iota(jnp.int32, (BAND, 1), 0)
        kind = jnp.where(t == 0, 0, 1) if blk == 0 else 1
        cols = []
        for g, w in enumerate(POOL_WINDOWS):
            c0 = g * gw
            ug = u_bf[r0:r0 + BAND, c0:c0 + gw]
            wsum = _dot(band_ref[kind, g], ug)
            top = wsum[:HALO] + _dot(bandh_ref[g], halo[:, c0:c0 + gw])
            wsum = jnp.concatenate([top, wsum[HALO:]], axis=0)
            cnt = jnp.minimum(pos + 1, w).astype(F32)
            pooled = wsum / cnt
            mixed = _dot(pooled.astype(BF16), grpw_ref[g]) * ascale_ref[:, c0:c0 + gw]
            cols.append((mixed * _silu(z[r0:r0 + BAND, c0:c0 + gw])).astype(BF16))
        gated_blocks.append(jnp.concatenate(cols, axis=1))
    gated = gated_blocks[0] if len(gated_blocks) == 1 else jnp.concatenate(gated_blocks, axis=0)

    y = _dot(gated, outw_ref[...])
    x1_ref[0] = x + gate_ref[0] * (_rms(y) * postg_ref[...])

    halo_ref[...] = u_bf[tile - HALO:]

    @pl.when(t == pl.num_programs(1) - 1)
    def _():
        tail_ref[0] = u[tile - HALO:]


def _pool_prompt(x, shift, scale, gate, pre_g, post_g, in_w, grp_w, a_scale, out_w, *, tile=512):
    b, s, d = x.shape
    w_a = out_w.shape[0]
    band, bandh = _band_constants()
    tok = pl.BlockSpec((1, tile, d), lambda i, j: (i, j, 0))
    mod = pl.BlockSpec((1, 1, d), lambda i, j: (i, 0, 0))
    return pl.pallas_call(
        functools.partial(_pool_prompt_kernel, tile=tile, w_a=w_a),
        grid=(b, s // tile),
        in_specs=[tok, mod, mod, mod, _const_spec((1, d)), _const_spec((1, d)),
                  _const_spec(in_w.shape), _const_spec(band.shape), _const_spec(bandh.shape),
                  _const_spec(grp_w.shape), _const_spec((1, w_a)), _const_spec(out_w.shape)],
        out_specs=[tok, pl.BlockSpec((1, HALO, w_a), lambda i, j: (i, 0, 0))],
        out_shape=[jax.ShapeDtypeStruct((b, s, d), F32),
                   jax.ShapeDtypeStruct((b, HALO, w_a), F32)],
        scratch_shapes=[pltpu.VMEM((HALO, w_a), BF16)],
        compiler_params=_params("arbitrary", "arbitrary"),
        name="pool_prompt",
    )(x, shift, scale, gate, pre_g, post_g, in_w, band, bandh, grp_w, a_scale, out_w)


def _pool_sample_kernel(x_ref, shift_ref, scale_ref, gate_ref, preg_ref, postg_ref, inw_ref,
                        hist_ref, grpw_ref, ascale_ref, outw_ref, x1_ref, u_ref, *, steps, w_a):
    gw = w_a // len(POOL_WINDOWS)
    rows = x_ref.shape[0]
    nh = hist_ref.shape[0]

    x = x_ref[...]
    h = (_rms(x) * preg_ref[...]) * (1.0 + scale_ref[...]) + shift_ref[...]
    uz = _dot(h.astype(BF16), inw_ref[...])
    u = uz[:, :w_a]
    z = uz[:, w_a:]
    u_ref[...] = u
    u_bf = u.astype(BF16)
    hist_bf = hist_ref[...].astype(BF16)

    n_batch = nh // POOL_BUF
    ro = lax.broadcasted_iota(jnp.int32, (rows, rows), 0)
    ri = lax.broadcasted_iota(jnp.int32, (rows, rows), 1)
    same = _div(ro, steps) == _div(ri, steps)
    lag_new = _mod(ro, steps) - _mod(ri, steps)
    roh = lax.broadcasted_iota(jnp.int32, (rows, nh), 0)
    ch = lax.broadcasted_iota(jnp.int32, (rows, nh), 1)
    in_batch = _mod(ch, n_batch) == _div(roh, steps)
    lag_hist = POOL_BUF + _mod(roh, steps) - _div(ch, n_batch)

    cols = []
    for g, w in enumerate(POOL_WINDOWS):
        c0 = g * gw
        sel_new = jnp.where(same & (lag_new >= 1) & (lag_new < w), 1.0, 0.0)
        sel_new = jnp.where(ro == ri, -(w - 1.0), sel_new).astype(BF16)
        sel_hist = jnp.where(in_batch & (lag_hist < w), 1.0, 0.0).astype(BF16)
        wsum = _dot(sel_new, u_bf[:, c0:c0 + gw]) + _dot(sel_hist, hist_bf[:, c0:c0 + gw])
        pooled = wsum / float(w)
        mixed = _dot(pooled.astype(BF16), grpw_ref[g]) * ascale_ref[:, c0:c0 + gw]
        cols.append((mixed * _silu(z[:, c0:c0 + gw])).astype(BF16))
    gated = jnp.concatenate(cols, axis=1)
    y = _dot(gated, outw_ref[...])
    x1_ref[...] = x + gate_ref[...] * (_rms(y) * postg_ref[...])


def _pool_sample(x, shift, scale, gate, pre_g, post_g, in_w, hist, grp_w, a_scale, out_w, *, steps):
    rows, d = x.shape
    w_a = out_w.shape[0]
    args = (x, shift, scale, gate, pre_g, post_g, in_w, hist, grp_w, a_scale, out_w)
    return pl.pallas_call(
        functools.partial(_pool_sample_kernel, steps=steps, w_a=w_a),
        grid=(1,),
        in_specs=[_const_spec(a.shape) for a in args],
        out_specs=[pl.BlockSpec((rows, d), lambda i: (0, 0)),
                   pl.BlockSpec((rows, w_a), lambda i: (0, 0))],
        out_shape=[jax.ShapeDtypeStruct((rows, d), F32),
                   jax.ShapeDtypeStruct((rows, w_a), F32)],
        compiler_params=_params("arbitrary"),
        name="pool_sample",
    )(*args)


def _proj_kernel(x_ref, shift_ref, scale_ref, kvg_ref, preg_ref, kvw_ref, wf_ref, fb_ref, binw_ref,
                 *rest, tile, d, prompt):
    if prompt:
        ltri_ref, place_ref, k_ref, vt_ref, sz_ref, kb_ref, vtb_ref, qtb_ref, fb16_ref, lft_ref, carry_ref = rest
    else:
        k_ref, v_ref, sz_ref, logf_ref, q_ref = rest
    n_heads = d // HEAD_DIM

    r = _rms(x_ref[0])
    xn = (r * kvg_ref[...]).astype(BF16)
    h = ((r * preg_ref[...]) * (1.0 + scale_ref[0]) + shift_ref[0]).astype(BF16)

    kv = _dot(xn, kvw_ref[...])
    k = kv[:, :d]
    v = kv[:, d:]
    k_ref[0] = k
    logf = _log_sigmoid(_dot(xn, wf_ref[...]) + fb_ref[...])

    qz = _dot(h, binw_ref[...])
    sz_ref[0] = _silu(qz[:, d:]).astype(BF16)

    if prompt:
        qt = (qz[:, :d] * (HEAD_DIM ** -0.5 * LOG2E)).T
        vt = v.T
        vt_ref[0] = vt
        for hp in range(d // LANES):
            sl = slice(hp * LANES, (hp + 1) * LANES)
            kb_ref[0, hp] = k[:, sl].astype(BF16)
            vtb_ref[0, hp] = vt[sl].astype(BF16)
            qtb_ref[0, hp] = qt[sl].astype(BF16)

        @pl.when(pl.program_id(1) == 0)
        def _():
            carry_ref[...] = jnp.zeros_like(carry_ref)

        lft_ref[0] = logf.T[:n_heads]
        cum = _dot_exact_rhs(ltri_ref[...], logf) + carry_ref[...]
        carry_ref[...] = cum[tile - 1:tile]
        hi, mid, lo = _split3(cum * LOG2E)
        fb16_ref[0] = (_dot(hi, place_ref[0]) + _dot(mid, place_ref[1]) + _dot(lo, place_ref[2])).astype(BF16)
    else:
        v_ref[0] = v
        logf_ref[0] = logf[:, :n_heads]
        q_ref[0] = qz[:, :d] * (HEAD_DIM ** -0.5)


def _proj(x, shift, scale, kv_g, pre_g, kv_w_main, w_f, f_b, b_in_w, *, tile, prompt):
    b, s, d = x.shape
    n_heads = d // HEAD_DIM
    hp = d // LANES
    tok = pl.BlockSpec((1, tile, d), lambda i, j: (i, j, 0))
    mod_rows = shift.shape[1]
    mod_tile = tile if mod_rows == s else 1
    mod = pl.BlockSpec((1, mod_tile, d), (lambda i, j: (i, j, 0)) if mod_rows == s else (lambda i, j: (i, 0, 0)))
    in_specs = [tok, mod, mod, _const_spec((1, d)), _const_spec((1, d)), _const_spec(kv_w_main.shape),
                _const_spec(w_f.shape), _const_spec(f_b.shape), _const_spec(b_in_w.shape)]
    args = [x, shift, scale, kv_g, pre_g, kv_w_main, w_f, f_b, b_in_w]
    scratch = []
    if prompt:
        ltri = jnp.asarray(np.tril(np.ones((tile, tile), np.float32)), BF16)
        place = np.zeros((3, LANES, LANES), np.float32)
        for term in range(3):
            place[term, np.arange(n_heads), term * n_heads + np.arange(n_heads)] = 1.0
        place = jnp.asarray(place, BF16)
        in_specs += [_const_spec(ltri.shape), _const_spec(place.shape)]
        args += [ltri, place]
        pair = pl.BlockSpec((1, hp, tile, LANES), lambda i, j: (i, 0, j, 0))
        pair_t = pl.BlockSpec((1, hp, LANES, tile), lambda i, j: (i, 0, 0, j))
        out_specs = [tok, pl.BlockSpec((1, d, tile), lambda i, j: (i, 0, j)), tok, pair, pair_t, pair_t,
                     pl.BlockSpec((1, tile, LANES), lambda i, j: (i, j, 0)),
                     pl.BlockSpec((1, n_heads, tile), lambda i, j: (i, 0, j))]
        out_shape = [jax.ShapeDtypeStruct((b, s, d), F32), jax.ShapeDtypeStruct((b, d, s), F32),
                     jax.ShapeDtypeStruct((b, s, d), BF16), jax.ShapeDtypeStruct((b, hp, s, LANES), BF16),
                     jax.ShapeDtypeStruct((b, hp, LANES, s), BF16), jax.ShapeDtypeStruct((b, hp, LANES, s), BF16),
                     jax.ShapeDtypeStruct((b, s, LANES), BF16), jax.ShapeDtypeStruct((b, n_heads, s), F32)]
        scratch = [pltpu.VMEM((1, LANES), F32)]
    else:
        out_specs = [tok, tok, tok, pl.BlockSpec((1, tile, n_heads), lambda i, j: (i, j, 0)), tok]
        out_shape = [jax.ShapeDtypeStruct((b, s, d), F32), jax.ShapeDtypeStruct((b, s, d), F32),
                     jax.ShapeDtypeStruct((b, s, d), BF16), jax.ShapeDtypeStruct((b, s, n_heads), F32),
                     jax.ShapeDtypeStruct((b, s, d), F32)]
    return pl.pallas_call(
        functools.partial(_proj_kernel, tile=tile, d=d, prompt=prompt),
        grid=(b, s // tile),
        in_specs=in_specs,
        out_specs=out_specs,
        out_shape=out_shape,
        scratch_shapes=scratch,
        compiler_params=_params("arbitrary", "arbitrary"),
        name="proj_prompt" if prompt else "proj_sample",
    )(*args)


def _prompt_tile(qt_ref, k_ref, f_ref, vt_ref, o_ref, sa_ref, sb_ref, pa_ref, pb_ref, *, tq, n_heads,
                 early_work, late_work):
    tk = tq
    pair = pl.program_id(1)
    qi = pl.program_id(2)
    qt = qt_ref[0, 0]
    row = lax.broadcasted_iota(jnp.int32, (LANES, 1), 0)
    zero = jnp.zeros_like(qt)
    qat = []
    for hd in range(2):
        head = 2 * pair + hd
        mine = (row >= hd * HEAD_DIM) & (row < (hd + 1) * HEAD_DIM)
        pick = (row == head) | (row == n_heads + head) | (row == 2 * n_heads + head)
        minus = jnp.where(pick, -1.0, 0.0).astype(BF16)
        qat.append(jnp.concatenate([jnp.where(mine, qt, zero), jnp.broadcast_to(minus, (LANES, tq))], axis=0))

    def qk(tile, s_ref):
        k0 = pl.multiple_of(tile * tk, tk)
        ka = jnp.concatenate([k_ref[0, 0, pl.ds(k0, tk), :], f_ref[0, pl.ds(k0, tk), :]], axis=1)
        for hd in range(2):
            s_ref[hd] = _dot(ka, qat[hd])

    ones = jnp.ones((8, tk), BF16)

    def pv(tile, p_ref):
        k0 = pl.multiple_of(tile * tk, tk)
        vt = vt_ref[0, 0, :, pl.ds(k0, tk)]
        return [_dot(jnp.concatenate([vt[hd * HEAD_DIM:(hd + 1) * HEAD_DIM], ones], axis=0), p_ref[hd])
                for hd in range(2)]

    def softmax(s_ref, p_ref, stats, diagonal=False):
        new_stats, alphas = [], []

        def scores(hd, r0):
            st = s_ref[hd, r0:r0 + STRIP, :]
            if diagonal:
                key = lax.broadcasted_iota(jnp.int32, (STRIP, tq), 0) + r0
                qry = lax.broadcasted_iota(jnp.int32, (STRIP, tq), 1)
                st = jnp.where(key <= qry, st, NEG)
            return st

        def fold(x, op):
            return op(x.reshape(STRIP // 8, 8, tq), axis=0)

        for hd in range(2):
            m = stats[hd]
            top = jnp.full((8, tq), NEG, F32)
            for r0 in range(0, tk, STRIP):
                top = jnp.maximum(top, fold(scores(hd, r0), jnp.max))
            m_new = jnp.maximum(m, jnp.max(top, axis=0, keepdims=True))
            alphas.append(jnp.exp2(m - m_new))
            for r0 in range(0, tk, STRIP):
                p_ref[hd, r0:r0 + STRIP, :] = jnp.exp2((scores(hd, r0) - m_new).astype(BF16))
            new_stats.append(m_new)
        return new_stats, alphas

    def rescale_add(alpha, acc, part):
        return [alpha[hd] * acc[hd] + part[hd] for hd in range(2)]

    qk(0, sa_ref)
    early_work()
    pb_ref[...] = jnp.zeros_like(pb_ref)
    stats = [jnp.full((1, tq), NEG, F32) for _ in range(2)]
    alpha_b = [jnp.ones((1, tq), F32) for _ in range(2)]
    acc = [jnp.zeros((HEAD_DIM + 8, tq), F32) for _ in range(2)]

    def body(t, carry):
        stats, alpha_b, acc = carry
        a = 2 * t
        part = pv(jnp.maximum(a - 1, 0), pb_ref)
        qk(a + 1, sb_ref)
        stats, alpha_a = softmax(sa_ref, pa_ref, stats)
        acc = rescale_add(alpha_b, acc, part)
        part = pv(a, pa_ref)
        qk(a + 2, sa_ref)
        stats, alpha_b = softmax(sb_ref, pb_ref, stats)
        acc = rescale_add(alpha_a, acc, part)
        return stats, alpha_b, acc

    pairs = qi // 2
    stats, alpha_b, acc = lax.fori_loop(0, pairs, body, (stats, alpha_b, acc))
    a = 2 * pairs

    def finish(acc):
        o_ref[0] = jnp.concatenate(
            [acc[hd][:HEAD_DIM] / acc[hd][HEAD_DIM:HEAD_DIM + 1] for hd in range(2)], axis=0).T

    @pl.when(a == qi)
    def _():
        part = pv(jnp.maximum(a - 1, 0), pb_ref)
        late_work()
        _, alpha_a = softmax(sa_ref, pa_ref, stats, diagonal=True)
        out = rescale_add(alpha_b, acc, part)
        finish(rescale_add(alpha_a, out, pv(a, pa_ref)))

    @pl.when(a != qi)
    def _():
        part = pv(jnp.maximum(a - 1, 0), pb_ref)
        qk(a + 1, sb_ref)
        late_work()
        mid, alpha_a = softmax(sa_ref, pa_ref, stats)
        out = rescale_add(alpha_b, acc, part)
        part = pv(a, pa_ref)
        _, alpha_d = softmax(sb_ref, pb_ref, mid, diagonal=True)
        out = rescale_add(alpha_a, out, part)
        finish(rescale_add(alpha_d, out, pv(a + 1, pb_ref)))


def _forget_prefix(blocks, carry):
    n_heads, page = blocks[0].shape
    stacked = jnp.concatenate(blocks, axis=0)
    r = lax.broadcasted_iota(jnp.int32, (page, page), 0)
    c = lax.broadcasted_iota(jnp.int32, (page, page), 1)
    within = _dot_exact_lhs(stacked, jnp.where(r <= c, 1.0, 0.0).astype(BF16))
    sums = []
    for b in range(len(blocks)):
        blk = within[b * n_heads:(b + 1) * n_heads] + carry
        sums.append(blk)
        carry = blk[:, page - 1:page]
    return sums, carry


def _sample_chunk(chunk, last, very_first, q_ref, knew_ref, vnew_ref, lfnew_ref, k_pages, v_pages, f_pages,
                  o_ref, qbd_ref, st_ref, acc_ref, m_ref, l_ref, fc_ref, knew_buf, vnew_buf, lfnew_buf,
                  *, steps, d):
    n_heads = d // HEAD_DIM
    page = k_pages[0].shape[2]
    rows = n_heads * steps
    assert page == LANES and rows % 8 == 0 and 3 * n_heads <= LANES

    row_id = lax.broadcasted_iota(jnp.int32, (rows, 1), 0)
    row_head = _div(row_id, steps)
    row_step = _mod(row_id, steps)
    lane_head = _div(lax.broadcasted_iota(jnp.int32, (1, d), 1), HEAD_DIM)

    @pl.when(very_first)
    def _():
        knew_buf[...] = jnp.zeros_like(knew_buf)
        vnew_buf[...] = jnp.zeros_like(vnew_buf)
        lfnew_buf[...] = jnp.zeros_like(lfnew_buf)

    @pl.when(chunk == 0)
    def _():
        q = q_ref[0]
        qbd = jnp.zeros((rows, d), F32)
        for i in range(steps):
            qbd = jnp.where((lane_head == row_head) & (row_step == i), q[i:i + 1, :], qbd)
        lane = lax.broadcasted_iota(jnp.int32, (1, LANES), 1)
        pick = (lane == row_head) | (lane == n_heads + row_head) | (lane == 2 * n_heads + row_head)
        qbd_ref[...] = jnp.concatenate([qbd, jnp.where(pick, -1.0, 0.0)], axis=1)
        m_ref[...] = jnp.full_like(m_ref, NEG)
        l_ref[...] = jnp.zeros_like(l_ref)
        acc_ref[...] = jnp.zeros_like(acc_ref)
        fc_ref[...] = jnp.zeros_like(fc_ref)

    def online_update(st, value_dot):
        m = m_ref[...]
        m_new = jnp.maximum(m, jnp.max(st, axis=1, keepdims=True))
        alpha = jnp.exp(m - m_new)
        p = jnp.exp(st - m_new)
        l_ref[...] = alpha * l_ref[...] + jnp.sum(p, axis=1, keepdims=True)
        m_ref[...] = m_new
        acc_ref[...] = alpha * acc_ref[...] + value_dot(p)

    def scores():
        sums, carry = _forget_prefix([r[0] for r in f_pages], fc_ref[...])
        fc_ref[...] = carry
        hi, mid, lo = _split3(jnp.concatenate(sums, axis=1))
        keys = hi.shape[1]
        terms = jnp.concatenate([hi.astype(F32), mid.astype(F32), lo.astype(F32),
                                 jnp.zeros((LANES - 3 * n_heads, keys), F32)], axis=0)
        kt = jnp.concatenate([jnp.concatenate([r[0] for r in k_pages], axis=1), terms], axis=0)
        st_ref[...] = _dot(qbd_ref[...], kt)

    def update():
        vt = jnp.concatenate([r[0] for r in v_pages], axis=1)
        online_update(st_ref[...], lambda p: _dot_nt(p, vt))

    def finish():
        @pl.when(chunk == last)
        def _():
            knew_buf[0:steps, :] = knew_ref[0]
            vnew_buf[0:steps, :] = vnew_ref[0]
            lfnew_buf[0:steps, 0:n_heads] = lfnew_ref[0]
            new_sums, _ = _forget_prefix([lfnew_buf[...].T[:n_heads]], fc_ref[...])
            expand = jnp.where(lax.broadcasted_iota(jnp.int32, (rows, n_heads), 1) == row_head,
                               1.0, 0.0).astype(BF16)
            st = _dot_nt(qbd_ref[:, :d], knew_buf[...]) - _dot_exact_rhs(expand, new_sums[0])
            key = lax.broadcasted_iota(jnp.int32, (1, LANES), 1)
            online_update(jnp.where(key <= row_step, st, NEG), lambda p: _dot(p, vnew_buf[...]))

            res = jnp.where(lane_head == row_head, acc_ref[...] / l_ref[...], 0.0)
            gather = jnp.where(lax.broadcasted_iota(jnp.int32, (8, rows), 0) ==
                               _mod(lax.broadcasted_iota(jnp.int32, (8, rows), 1), steps), 1.0, 0.0).astype(BF16)
            o_ref[0] = _dot_exact_rhs(gather, res)[:steps]

    return scores, update, finish


def _attn_kernel(pt_ref, qt_ref, k_ref, f_ref, vt_ref, qs_ref, knew_ref, vnew_ref, lfnew_ref, *rest,
                 tq, n_heads, pages, steps, d):
    del pt_ref
    k_pages, v_pages, f_pages = rest[:pages], rest[pages:2 * pages], rest[2 * pages:3 * pages]
    (o_ref, os_ref, sa_ref, sb_ref, pa_ref, pb_ref,
     qbd_ref, st_ref, acc_ref, m_ref, l_ref, fc_ref, knew_buf, vnew_buf, lfnew_buf) = rest[3 * pages:]
    chunk = pl.program_id(2)
    very_first = (pl.program_id(0) == 0) & (pl.program_id(1) == 0) & (chunk == 0)
    sample_scores, sample_update, sample_finish = _sample_chunk(
        chunk, pl.num_programs(2) - 1, very_first, qs_ref, knew_ref, vnew_ref, lfnew_ref,
        k_pages, v_pages, f_pages, os_ref, qbd_ref, st_ref, acc_ref, m_ref, l_ref, fc_ref,
        knew_buf, vnew_buf, lfnew_buf, steps=steps, d=d)
    _prompt_tile(qt_ref, k_ref, f_ref, vt_ref, o_ref, sa_ref, sb_ref, pa_ref, pb_ref, tq=tq, n_heads=n_heads,
                 early_work=sample_scores, late_work=sample_update)
    sample_finish()


def _attention(qtb, kb, fb16, vtb, page_table, q_s, k_new, v_new, lf_new, cache_k, cache_v, cache_logf,
               *, n_heads, tq=512):
    b, hp, s, _ = kb.shape
    nb, steps, d = q_s.shape
    n_pages = page_table.shape[1]
    n_phys, page, _ = cache_logf.shape
    rows = n_heads * steps
    n_tiles = s // tq
    assert nb == b * hp and n_pages % n_tiles == 0
    pages = n_pages // n_tiles
    cache_k = jnp.transpose(cache_k, (0, 2, 3, 1)).reshape(n_phys, d, page)
    cache_v = jnp.transpose(cache_v, (0, 2, 3, 1)).reshape(n_phys, d, page)
    cache_logf = jnp.transpose(cache_logf, (0, 2, 1))

    def page_map(r):
        return lambda i, p, j, pt: (pt[(i * hp + p) * n_pages + j * pages + r], 0, 0)

    batch = lambda i, p, j, pt: (i * hp + p, 0, 0)
    kv_specs = [pl.BlockSpec((1, d, page), page_map(r)) for r in range(pages)]
    f_specs = [pl.BlockSpec((1, n_heads, page), page_map(r)) for r in range(pages)]
    grid_spec = pltpu.PrefetchScalarGridSpec(
        num_scalar_prefetch=1,
        grid=(b, hp, n_tiles),
        in_specs=[pl.BlockSpec((1, 1, LANES, tq), lambda i, p, j, pt: (i, p, 0, j)),
                  pl.BlockSpec((1, 1, s, LANES), lambda i, p, j, pt: (i, p, 0, 0)),
                  pl.BlockSpec((1, s, LANES), lambda i, p, j, pt: (i, 0, 0)),
                  pl.BlockSpec((1, 1, LANES, s), lambda i, p, j, pt: (i, p, 0, 0)),
                  pl.BlockSpec((1, steps, d), batch), pl.BlockSpec((1, steps, d), batch),
                  pl.BlockSpec((1, steps, d), batch), pl.BlockSpec((1, steps, n_heads), batch)]
                 + kv_specs + kv_specs + f_specs,
        out_specs=[pl.BlockSpec((1, tq, LANES), lambda i, p, j, pt: (i, j, p)),
                   pl.BlockSpec((1, steps, d), batch)],
        scratch_shapes=[pltpu.VMEM((2, tq, tq), F32), pltpu.VMEM((2, tq, tq), F32),
                        pltpu.VMEM((2, tq, tq), BF16), pltpu.VMEM((2, tq, tq), BF16),
                        pltpu.VMEM((rows, d + LANES), F32), pltpu.VMEM((rows, pages * page), F32),
                        pltpu.VMEM((rows, d), F32),
                        pltpu.VMEM((rows, 1), F32), pltpu.VMEM((rows, 1), F32),
                        pltpu.VMEM((n_heads, 1), F32), pltpu.VMEM((LANES, d), F32),
                        pltpu.VMEM((LANES, d), F32), pltpu.VMEM((LANES, LANES), F32)],
    )
    return pl.pallas_call(
        functools.partial(_attn_kernel, tq=tq, n_heads=n_heads, pages=pages, steps=steps, d=d),
        grid_spec=grid_spec,
        out_shape=[jax.ShapeDtypeStruct((b, s, hp * LANES), F32), jax.ShapeDtypeStruct((nb, steps, d), F32)],
        compiler_params=_params("arbitrary", "arbitrary", "arbitrary"),
        name="attention",
    )(page_table.reshape(-1), qtb, kb, fb16, vtb, q_s, k_new, v_new, lf_new,
      *([cache_k] * pages), *([cache_v] * pages), *([cache_logf] * pages))


def _outproj_kernel(o_ref, sz_ref, w_ref, postg_ref, gate_ref, x_ref, y_ref):
    g = (o_ref[0] * sz_ref[0].astype(F32)).astype(BF16)
    out = _dot(g, w_ref[...])
    y_ref[0] = x_ref[0] + gate_ref[0] * (_rms(out) * postg_ref[...])


def _outproj(o, sz, w, post_g, gate, x, *, tile, name):
    b, s, d = x.shape
    tok = pl.BlockSpec((1, tile, d), lambda i, j: (i, j, 0))
    per_row = gate.shape[1] == s
    mod = pl.BlockSpec((1, tile if per_row else 1, d),
                       (lambda i, j: (i, j, 0)) if per_row else (lambda i, j: (i, 0, 0)))
    return pl.pallas_call(
        _outproj_kernel,
        grid=(b, s // tile),
        in_specs=[tok, tok, _const_spec(w.shape), _const_spec((1, d)), mod, tok],
        out_specs=tok,
        out_shape=jax.ShapeDtypeStruct((b, s, d), F32),
        compiler_params=_params("arbitrary", "arbitrary"),
        name=name,
    )(o, sz, w, post_g, gate, x)


def kernel(x_prompt, x_sample, state_pool, cache_k, cache_v, cache_logf, page_table, c_prompt, c_sample,
           ada_w, ada_b, pre_g, post_g, a_in_w, a_grp_w, a_scale, a_out_w, kv_g, kv_w, f_b, b_in_w, b_out_w):
    bp, seq, d = x_prompt.shape
    bs, steps, _ = x_sample.shape
    n_heads = d // HEAD_DIM
    w_a = a_out_w.shape[1]
    rows_s = bs * steps

    pad = (-(bp + bs)) % 8
    c_all = jnp.concatenate([c_prompt, c_sample, jnp.zeros((pad, d), F32)], axis=0)
    mod = _adaln(c_all, ada_w, ada_b)

    def mods(layer):
        m = mod[layer]
        parts = [m[:, i * d:(i + 1) * d] for i in range(3)]
        prompt = [p[:bp].reshape(bp, 1, d) for p in parts]
        sample = [jnp.repeat(p[bp:bp + bs], steps, axis=0) for p in parts]
        return prompt, sample

    (shift0_p, scale0_p, gate0_p), (shift0_s, scale0_s, gate0_s) = mods(0)
    (shift1_p, scale1_p, gate1_p), (shift1_s, scale1_s, gate1_s) = mods(1)

    in_w = a_in_w[0].astype(BF16)
    grp_w = a_grp_w[0].astype(BF16)
    out_w = a_out_w[0].astype(BF16)
    kv_w_main = kv_w[:, :2 * d].astype(BF16)
    w_f = jnp.pad(kv_w[:, 2 * d:], ((0, 0), (0, LANES - n_heads))).astype(BF16)
    f_b_pad = jnp.pad(f_b, (0, LANES - n_heads)).reshape(1, LANES)
    bin_w = b_in_w[0].astype(BF16)
    bout_w = b_out_w[0].astype(BF16)
    pre0, pre1 = pre_g[0].reshape(1, d), pre_g[1].reshape(1, d)
    post0, post1 = post_g[0].reshape(1, d), post_g[1].reshape(1, d)
    kvg = kv_g.reshape(1, d)
    asc = a_scale[0].reshape(1, w_a)

    hist = jnp.transpose(state_pool[0], (1, 0, 2)).reshape(POOL_BUF * bs, w_a)
    x1_s, u_s = _pool_sample(x_sample.reshape(rows_s, d), shift0_s, scale0_s, gate0_s, pre0, post0,
                             in_w, hist, grp_w, asc, out_w, steps=steps)
    pool_sample = jnp.concatenate([state_pool[:, :, steps:], u_s.reshape(1, bs, steps, w_a)], axis=2)
    x1_s3 = x1_s.reshape(1, rows_s, d)
    k_s, v_s, sz_s, logf_s, q_s = _proj(
        x1_s3, shift1_s[None], scale1_s[None], kvg, pre1, kv_w_main, w_f, f_b_pad, bin_w,
        tile=rows_s, prompt=False)

    x1_p, tail_p = _pool_prompt(x_prompt, shift0_p, scale0_p, gate0_p, pre0, post0, in_w, grp_w, asc, out_w)
    pool_prompt = tail_p[None, :, HALO - POOL_BUF:, :]
    k_p, vt_p, sz_p, kb, vtb, qtb, fb16, lft_p = _proj(
        x1_p, shift1_p, scale1_p, kvg, pre1, kv_w_main, w_f, f_b_pad, bin_w, tile=512, prompt=True)
    logf_p = jnp.transpose(lft_p, (0, 2, 1))
    v_p = jnp.transpose(vt_p.reshape(bp, n_heads, HEAD_DIM, seq), (0, 3, 1, 2))

    o_p, o_s = _attention(qtb, kb, fb16, vtb, page_table, q_s.reshape(bs, steps, d), k_s.reshape(bs, steps, d),
                          v_s.reshape(bs, steps, d), logf_s.reshape(bs, steps, n_heads),
                          cache_k, cache_v, cache_logf, n_heads=n_heads)
    y_prompt = _outproj(o_p, sz_p, bout_w, post1, gate1_p, x1_p, tile=512, name="outproj_prompt")
    y_sample = _outproj(o_s.reshape(1, rows_s, d), sz_s, bout_w, post1, gate1_s[None], x1_s3,
                        tile=rows_s, name="outproj_sample")

    return (y_prompt, y_sample.reshape(bs, steps, d), pool_prompt, pool_sample,
            k_p.reshape(bp, seq, n_heads, HEAD_DIM), v_p, logf_p,
            k_s.reshape(bs, steps, n_heads, HEAD_DIM), v_s.reshape(bs, steps, n_heads, HEAD_DIM),
            logf_s.reshape(bs, steps, n_heads))
```

```python
import functools

import jax
import jax.numpy as jnp
import numpy as np
from jax import lax
from jax.experimental import pallas as pl
from jax.experimental.pallas import tpu as pltpu

F32 = jnp.float32
BF16 = jnp.bfloat16

EPS = 1e-6
NEG = -1e30
LOG2E = 1.4426950408889634
POOL_WINDOWS = (2, 4, 8, 16)
POOL_BUF = max(POOL_WINDOWS) - 1
HEAD_DIM = 64
HALO = 16

V7X_VMEM_BYTES = 64 * 1024 * 1024
VMEM_LIMIT = V7X_VMEM_BYTES - 8 * 1024 * 1024
LANES = 128
BAND = 256
STRIP = 32


def _params(*sem):
    return pltpu.CompilerParams(dimension_semantics=sem, vmem_limit_bytes=VMEM_LIMIT)


def _const_spec(shape):
    nd = len(shape)
    return pl.BlockSpec(shape, lambda *_: (0,) * nd, pipeline_mode=pl.Buffered(1))


def _silu(x):
    return x * jax.nn.sigmoid(x)


def _rms(x):
    return x * lax.rsqrt(jnp.mean(x * x, axis=-1, keepdims=True) + EPS)


def _split3(x):
    a = x.astype(BF16)
    r = x - a.astype(F32)
    b = r.astype(BF16)
    c = (r - b.astype(F32)).astype(BF16)
    return a, b, c


def _log_sigmoid(x):
    return jnp.minimum(x, 0.0) - jnp.log1p(jnp.exp(-jnp.abs(x)))


def _div(x, n):
    return x >> (n.bit_length() - 1) if n & (n - 1) == 0 else x // n


def _mod(x, n):
    return x & (n - 1) if n & (n - 1) == 0 else x % n


def _dot(a, b):
    return jnp.dot(a, b, preferred_element_type=F32)


def _dot_nt(a, b):
    return lax.dot_general(a, b, (((1,), (1,)), ((), ())), preferred_element_type=F32)


def _dot_exact_rhs(sel, x):
    a, b, c = _split3(x)
    return _dot(sel, a) + _dot(sel, b) + _dot(sel, c)


def _dot_exact_lhs(x, sel):
    a, b, c = _split3(x)
    return _dot(a, sel) + _dot(b, sel) + _dot(c, sel)


def _adaln_kernel(c_ref, w_ref, b_ref, o_ref):
    a = _silu(c_ref[...]).astype(BF16)
    o_ref[0] = _dot(a, w_ref[0].astype(BF16)) + b_ref[0]


def _adaln(c_all, ada_w, ada_b):
    depth, d, d3 = ada_w.shape
    rows = c_all.shape[0]
    tn = d3 // 2
    return pl.pallas_call(
        _adaln_kernel,
        grid=(depth, d3 // tn),
        in_specs=[
            pl.BlockSpec((rows, d), lambda l, j: (0, 0)),
            pl.BlockSpec((1, d, tn), lambda l, j: (l, 0, j)),
            pl.BlockSpec((1, 1, tn), lambda l, j: (l, 0, j)),
        ],
        out_specs=pl.BlockSpec((1, rows, tn), lambda l, j: (l, 0, j)),
        out_shape=jax.ShapeDtypeStruct((depth, rows, d3), F32),
        compiler_params=_params("arbitrary", "arbitrary"),
        name="adaln",
    )(c_all, ada_w, ada_b.reshape(depth, 1, d3))


def _band_constants():
    t = np.arange(BAND)[:, None]
    s = np.arange(BAND)[None, :]
    eye = (t == s).astype(np.float32)

    def one(w, first):
        cnt = np.minimum(t + 1, w) if first else w
        return ((t - s >= 1) & (t - s < w)).astype(np.float32) - (cnt - 1) * eye

    band = np.stack([np.stack([one(w, first) for w in POOL_WINDOWS]) for first in (True, False)])
    th = np.arange(HALO)[:, None]
    jh = np.arange(HALO)[None, :]
    bandh = np.stack([(th - jh + HALO < w) for w in POOL_WINDOWS]).astype(np.float32)
    return jnp.asarray(band, BF16), jnp.asarray(bandh, BF16)


def _pool_prompt_kernel(x_ref, shift_ref, scale_ref, gate_ref, preg_ref, postg_ref, inw_ref,
                        band_ref, bandh_ref, grpw_ref, ascale_ref, outw_ref,
                        x1_ref, tail_ref, halo_ref, *, tile, w_a):
    t = pl.program_id(1)
    gw = w_a // len(POOL_WINDOWS)

    @pl.when(t == 0)
    def _():
        halo_ref[...] = jnp.zeros_like(halo_ref)

    x = x_ref[0]
    h = (_rms(x) * preg_ref[...]) * (1.0 + scale_ref[0]) + shift_ref[0]
    uz = _dot(h.astype(BF16), inw_ref[...])
    u = uz[:, :w_a]
    z = uz[:, w_a:]
    u_bf = u.astype(BF16)

    gated_blocks = []
    for blk in range(tile // BAND):
        r0 = blk * BAND
        halo = halo_ref[...] if blk == 0 else u_bf[r0 - HALO:r0]
        pos = t * tile + r0 + lax.broadcasted_iota(jnp.int32, (BAND, 1), 0)
        kind = jnp.where(t == 0, 0, 1) if blk == 0 else 1
        cols = []
        for g, w in enumerate(POOL_WINDOWS):
            c0 = g * gw
            ug = u_bf[r0:r0 + BAND, c0:c0 + gw]
            wsum = _dot(band_ref[kind, g], ug)
            top = wsum[:HALO] + _dot(bandh_ref[g], halo[:, c0:c0 + gw])
            wsum = jnp.concatenate([top, wsum[HALO:]], axis=0)
            cnt = jnp.minimum(pos + 1, w).astype(F32)
            pooled = wsum / cnt
            mixed = _dot(pooled.astype(BF16), grpw_ref[g]) * ascale_ref[:, c0:c0 + gw]
            cols.append((mixed * _silu(z[r0:r0 + BAND, c0:c0 + gw])).astype(BF16))
        gated_blocks.append(jnp.concatenate(cols, axis=1))
    gated = gated_blocks[0] if len(gated_blocks) == 1 else jnp.concatenate(gated_blocks, axis=0)

    y = _dot(gated, outw_ref[...])
    x1_ref[0] = x + gate_ref[0] * (_rms(y) * postg_ref[...])

    halo_ref[...] = u_bf[tile - HALO:]

    @pl.when(t == pl.num_programs(1) - 1)
    def _():
        tail_ref[0] = u[tile - HALO:]


def _pool_prompt(x, shift, scale, gate, pre_g, post_g, in_w, grp_w, a_scale, out_w, *, tile=512):
    b, s, d = x.shape
    w_a = out_w.shape[0]
    band, bandh = _band_constants()
    tok = pl.BlockSpec((1, tile, d), lambda i, j: (i, j, 0))
    mod = pl.BlockSpec((1, 1, d), lambda i, j: (i, 0, 0))
    return pl.pallas_call(
        functools.partial(_pool_prompt_kernel, tile=tile, w_a=w_a),
        grid=(b, s // tile),
        in_specs=[tok, mod, mod, mod, _const_spec((1, d)), _const_spec((1, d)),
                  _const_spec(in_w.shape), _const_spec(band.shape), _const_spec(bandh.shape),
                  _const_spec(grp_w.shape), _const_spec((1, w_a)), _const_spec(out_w.shape)],
        out_specs=[tok, pl.BlockSpec((1, HALO, w_a), lambda i, j: (i, 0, 0))],
        out_shape=[jax.ShapeDtypeStruct((b, s, d), F32),
                   jax.ShapeDtypeStruct((b, HALO, w_a), F32)],
        scratch_shapes=[pltpu.VMEM((HALO, w_a), BF16)],
        compiler_params=_params("arbitrary", "arbitrary"),
        name="pool_prompt",
    )(x, shift, scale, gate, pre_g, post_g, in_w, band, bandh, grp_w, a_scale, out_w)


def _pool_sample_kernel(x_ref, shift_ref, scale_ref, gate_ref, preg_ref, postg_ref, inw_ref,
                        hist_ref, grpw_ref, ascale_ref, outw_ref, x1_ref, u_ref, *, steps, w_a):
    gw = w_a // len(POOL_WINDOWS)
    rows = x_ref.shape[0]
    nh = hist_ref.shape[0]

    x = x_ref[...]
    h = (_rms(x) * preg_ref[...]) * (1.0 + scale_ref[...]) + shift_ref[...]
    uz = _dot(h.astype(BF16), inw_ref[...])
    u = uz[:, :w_a]
    z = uz[:, w_a:]
    u_ref[...] = u
    u_bf = u.astype(BF16)
    hist_bf = hist_ref[...].astype(BF16)

    n_batch = nh // POOL_BUF
    ro = lax.broadcasted_iota(jnp.int32, (rows, rows), 0)
    ri = lax.broadcasted_iota(jnp.int32, (rows, rows), 1)
    same = _div(ro, steps) == _div(ri, steps)
    lag_new = _mod(ro, steps) - _mod(ri, steps)
    roh = lax.broadcasted_iota(jnp.int32, (rows, nh), 0)
    ch = lax.broadcasted_iota(jnp.int32, (rows, nh), 1)
    in_batch = _mod(ch, n_batch) == _div(roh, steps)
    lag_hist = POOL_BUF + _mod(roh, steps) - _div(ch, n_batch)

    cols = []
    for g, w in enumerate(POOL_WINDOWS):
        c0 = g * gw
        sel_new = jnp.where(same & (lag_new >= 1) & (lag_new < w), 1.0, 0.0)
        sel_new = jnp.where(ro == ri, -(w - 1.0), sel_new).astype(BF16)
        sel_hist = jnp.where(in_batch & (lag_hist < w), 1.0, 0.0).astype(BF16)
        wsum = _dot(sel_new, u_bf[:, c0:c0 + gw]) + _dot(sel_hist, hist_bf[:, c0:c0 + gw])
        pooled = wsum / float(w)
        mixed = _dot(pooled.astype(BF16), grpw_ref[g]) * ascale_ref[:, c0:c0 + gw]
        cols.append((mixed * _silu(z[:, c0:c0 + gw])).astype(BF16))
    gated = jnp.concatenate(cols, axis=1)
    y = _dot(gated, outw_ref[...])
    x1_ref[...] = x + gate_ref[...] * (_rms(y) * postg_ref[...])


def _pool_sample(x, shift, scale, gate, pre_g, post_g, in_w, hist, grp_w, a_scale, out_w, *, steps):
    rows, d = x.shape
    w_a = out_w.shape[0]
    args = (x, shift, scale, gate, pre_g, post_g, in_w, hist, grp_w, a_scale, out_w)
    return pl.pallas_call(
        functools.partial(_pool_sample_kernel, steps=steps, w_a=w_a),
        grid=(1,),
        in_specs=[_const_spec(a.shape) for a in args],
        out_specs=[pl.BlockSpec((rows, d), lambda i: (0, 0)),
                   pl.BlockSpec((rows, w_a), lambda i: (0, 0))],
        out_shape=[jax.ShapeDtypeStruct((rows, d), F32),
                   jax.ShapeDtypeStruct((rows, w_a), F32)],
        compiler_params=_params("arbitrary"),
        name="pool_sample",
    )(*args)


def _proj_kernel(x_ref, shift_ref, scale_ref, kvg_ref, preg_ref, kvw_ref, wf_ref, fb_ref, binw_ref,
                 *rest, tile, d, prompt):
    if prompt:
        ltri_ref, place_ref, k_ref, vt_ref, sz_ref, kb_ref, vtb_ref, qtb_ref, fb16_ref, lft_ref, carry_ref = rest
    else:
        k_ref, v_ref, sz_ref, logf_ref, q_ref = rest
    n_heads = d // HEAD_DIM

    r = _rms(x_ref[0])
    xn = (r * kvg_ref[...]).astype(BF16)
    h = ((r * preg_ref[...]) * (1.0 + scale_ref[0]) + shift_ref[0]).astype(BF16)

    kv = _dot(xn, kvw_ref[...])
    k = kv[:, :d]
    v = kv[:, d:]
    k_ref[0] = k
    logf = _log_sigmoid(_dot(xn, wf_ref[...]) + fb_ref[...])

    qz = _dot(h, binw_ref[...])
    sz_ref[0] = _silu(qz[:, d:]).astype(BF16)

    if prompt:
        qt = (qz[:, :d] * (HEAD_DIM ** -0.5 * LOG2E)).T
        vt = v.T
        vt_ref[0] = vt
        for hp in range(d // LANES):
            sl = slice(hp * LANES, (hp + 1) * LANES)
            kb_ref[0, hp] = k[:, sl].astype(BF16)
            vtb_ref[0, hp] = vt[sl].astype(BF16)
            qtb_ref[0, hp] = qt[sl].astype(BF16)

        @pl.when(pl.program_id(1) == 0)
        def _():
            carry_ref[...] = jnp.zeros_like(carry_ref)

        lft_ref[0] = logf.T[:n_heads]
        cum = _dot_exact_rhs(ltri_ref[...], logf) + carry_ref[...]
        carry_ref[...] = cum[tile - 1:tile]
        hi, mid, lo = _split3(cum * LOG2E)
        fb16_ref[0] = (_dot(hi, place_ref[0]) + _dot(mid, place_ref[1]) + _dot(lo, place_ref[2])).astype(BF16)
    else:
        v_ref[0] = v
        logf_ref[0] = logf[:, :n_heads]
        q_ref[0] = qz[:, :d] * (HEAD_DIM ** -0.5)


def _proj(x, shift, scale, kv_g, pre_g, kv_w_main, w_f, f_b, b_in_w, *, tile, prompt):
    b, s, d = x.shape
    n_heads = d // HEAD_DIM
    hp = d // LANES
    tok = pl.BlockSpec((1, tile, d), lambda i, j: (i, j, 0))
    mod_rows = shift.shape[1]
    mod_tile = tile if mod_rows == s else 1
    mod = pl.BlockSpec((1, mod_tile, d), (lambda i, j: (i, j, 0)) if mod_rows == s else (lambda i, j: (i, 0, 0)))
    in_specs = [tok, mod, mod, _const_spec((1, d)), _const_spec((1, d)), _const_spec(kv_w_main.shape),
                _const_spec(w_f.shape), _const_spec(f_b.shape), _const_spec(b_in_w.shape)]
    args = [x, shift, scale, kv_g, pre_g, kv_w_main, w_f, f_b, b_in_w]
    scratch = []
    if prompt:
        ltri = jnp.asarray(np.tril(np.ones((tile, tile), np.float32)), BF16)
        place = np.zeros((3, LANES, LANES), np.float32)
        for term in range(3):
            place[term, np.arange(n_heads), term * n_heads + np.arange(n_heads)] = 1.0
        place = jnp.asarray(place, BF16)
        in_specs += [_const_spec(ltri.shape), _const_spec(place.shape)]
        args += [ltri, place]
        pair = pl.BlockSpec((1, hp, tile, LANES), lambda i, j: (i, 0, j, 0))
        pair_t = pl.BlockSpec((1, hp, LANES, tile), lambda i, j: (i, 0, 0, j))
        out_specs = [tok, pl.BlockSpec((1, d, tile), lambda i, j: (i, 0, j)), tok, pair, pair_t, pair_t,
                     pl.BlockSpec((1, tile, LANES), lambda i, j: (i, j, 0)),
                     pl.BlockSpec((1, n_heads, tile), lambda i, j: (i, 0, j))]
        out_shape = [jax.ShapeDtypeStruct((b, s, d), F32), jax.ShapeDtypeStruct((b, d, s), F32),
                     jax.ShapeDtypeStruct((b, s, d), BF16), jax.ShapeDtypeStruct((b, hp, s, LANES), BF16),
                     jax.ShapeDtypeStruct((b, hp, LANES, s), BF16), jax.ShapeDtypeStruct((b, hp, LANES, s), BF16),
                     jax.ShapeDtypeStruct((b, s, LANES), BF16), jax.ShapeDtypeStruct((b, n_heads, s), F32)]
        scratch = [pltpu.VMEM((1, LANES), F32)]
    else:
        out_specs = [tok, tok, tok, pl.BlockSpec((1, tile, n_heads), lambda i, j: (i, j, 0)), tok]
        out_shape = [jax.ShapeDtypeStruct((b, s, d), F32), jax.ShapeDtypeStruct((b, s, d), F32),
                     jax.ShapeDtypeStruct((b, s, d), BF16), jax.ShapeDtypeStruct((b, s, n_heads), F32),
                     jax.ShapeDtypeStruct((b, s, d), F32)]
    return pl.pallas_call(
        functools.partial(_proj_kernel, tile=tile, d=d, prompt=prompt),
        grid=(b, s // tile),
        in_specs=in_specs,
        out_specs=out_specs,
        out_shape=out_shape,
        scratch_shapes=scratch,
        compiler_params=_params("arbitrary", "arbitrary"),
        name="proj_prompt" if prompt else "proj_sample",
    )(*args)


def _prompt_tile(qt_ref, k_ref, f_ref, vt_ref, o_ref, sa_ref, sb_ref, pa_ref, pb_ref, *, tq, n_heads,
                 early_work, late_work):
    tk = tq
    pair = pl.program_id(1)
    qi = pl.program_id(2)
    qt = qt_ref[0, 0]
    row = lax.broadcasted_iota(jnp.int32, (LANES, 1), 0)
    zero = jnp.zeros_like(qt)
    qat = []
    for hd in range(2):
        head = 2 * pair + hd
        mine = (row >= hd * HEAD_DIM) & (row < (hd + 1) * HEAD_DIM)
        pick = (row == head) | (row == n_heads + head) | (row == 2 * n_heads + head)
        minus = jnp.where(pick, -1.0, 0.0).astype(BF16)
        qat.append(jnp.concatenate([jnp.where(mine, qt, zero), jnp.broadcast_to(minus, (LANES, tq))], axis=0))

    def qk(tile, s_ref):
        k0 = pl.multiple_of(tile * tk, tk)
        ka = jnp.concatenate([k_ref[0, 0, pl.ds(k0, tk), :], f_ref[0, pl.ds(k0, tk), :]], axis=1)
        for hd in range(2):
            s_ref[hd] = _dot(ka, qat[hd])

    half = tq // 2

    def qk_diagonal(tile, s_ref):
        k0 = pl.multiple_of(tile * tk, tk)
        ka = jnp.concatenate([k_ref[0, 0, pl.ds(k0, tk), :], f_ref[0, pl.ds(k0, tk), :]], axis=1)
        for hd in range(2):
            s_ref[hd, :half, :] = _dot(ka[:half], qat[hd])
            s_ref[hd, half:, half:] = _dot(ka[half:], qat[hd][:, half:])

    ones = jnp.ones((8, tk), BF16)

    def pv(tile, p_ref, diagonal=False):
        k0 = pl.multiple_of(tile * tk, tk)
        vt = vt_ref[0, 0, :, pl.ds(k0, tk)]
        parts = []
        for hd in range(2):
            va = jnp.concatenate([vt[hd * HEAD_DIM:(hd + 1) * HEAD_DIM], ones], axis=0)
            if diagonal:
                early = _dot(va[:, :half], p_ref[hd, :half, :])
                late = _dot(va[:, half:], p_ref[hd, half:, half:])
                parts.append(jnp.concatenate([early[:, :half], early[:, half:] + late], axis=1))
            else:
                parts.append(_dot(va, p_ref[hd]))
        return parts

    def softmax(s_ref, p_ref, stats, diagonal=False):
        new_stats, alphas = [], []

        def scores(hd, r0):
            c0 = half if diagonal and r0 >= half else 0
            st = s_ref[hd, r0:r0 + STRIP, c0:]
            if diagonal:
                key = lax.broadcasted_iota(jnp.int32, st.shape, 0) + r0
                qry = lax.broadcasted_iota(jnp.int32, st.shape, 1) + c0
                st = jnp.where(key <= qry, st, NEG)
            return st, c0

        def fold(x, op):
            return op(x.reshape(STRIP // 8, 8, x.shape[1]), axis=0)

        for hd in range(2):
            m = stats[hd]
            top = jnp.full((8, tq), NEG, F32)
            for r0 in range(0, tk, STRIP):
                st, c0 = scores(hd, r0)
                upd = jnp.maximum(top[:, c0:], fold(st, jnp.max))
                top = upd if c0 == 0 else jnp.concatenate([top[:, :c0], upd], axis=1)
            m_new = jnp.maximum(m, jnp.max(top, axis=0, keepdims=True))
            alphas.append(jnp.exp2(m - m_new))
            for r0 in range(0, tk, STRIP):
                st, c0 = scores(hd, r0)
                p_ref[hd, r0:r0 + STRIP, c0:] = jnp.exp2((st - m_new[:, c0:]).astype(BF16))
            new_stats.append(m_new)
        return new_stats, alphas

    def rescale_add(alpha, acc, part):
        return [alpha[hd] * acc[hd] + part[hd] for hd in range(2)]

    def pv_before(a):
        return [jnp.where(a > 0, part, 0.0) for part in pv(jnp.maximum(a - 1, 0), pb_ref)]

    @pl.when((pl.program_id(0) == 0) & (pair == 0) & (qi == 0))
    def _():
        pb_ref[...] = jnp.zeros_like(pb_ref)

    qk(0, sa_ref)
    early_work()
    stats = [jnp.full((1, tq), NEG, F32) for _ in range(2)]
    alpha_b = [jnp.ones((1, tq), F32) for _ in range(2)]
    acc = [jnp.zeros((HEAD_DIM + 8, tq), F32) for _ in range(2)]

    def body(t, carry):
        stats, alpha_b, acc = carry
        a = 2 * t
        part = pv_before(a)
        qk(a + 1, sb_ref)
        stats, alpha_a = softmax(sa_ref, pa_ref, stats)
        acc = rescale_add(alpha_b, acc, part)
        part = pv(a, pa_ref)
        qk(a + 2, sa_ref)
        stats, alpha_b = softmax(sb_ref, pb_ref, stats)
        acc = rescale_add(alpha_a, acc, part)
        return stats, alpha_b, acc

    pairs = qi // 2
    stats, alpha_b, acc = lax.fori_loop(0, pairs, body, (stats, alpha_b, acc))
    a = 2 * pairs

    def finish(acc):
        o_ref[0] = jnp.concatenate(
            [acc[hd][:HEAD_DIM] / acc[hd][HEAD_DIM:HEAD_DIM + 1] for hd in range(2)], axis=0).T.astype(o_ref.dtype)

    @pl.when(a == qi)
    def _():
        part = pv_before(a)
        late_work()
        _, alpha_a = softmax(sa_ref, pa_ref, stats, diagonal=True)
        out = rescale_add(alpha_b, acc, part)
        finish(rescale_add(alpha_a, out, pv(a, pa_ref, diagonal=True)))

    @pl.when(a != qi)
    def _():
        part = pv_before(a)
        qk_diagonal(a + 1, sb_ref)
        late_work()
        mid, alpha_a = softmax(sa_ref, pa_ref, stats)
        out = rescale_add(alpha_b, acc, part)
        part = pv(a, pa_ref)
        _, alpha_d = softmax(sb_ref, pb_ref, mid, diagonal=True)
        out = rescale_add(alpha_a, out, part)
        finish(rescale_add(alpha_d, out, pv(a + 1, pb_ref, diagonal=True)))


def _forget_prefix(blocks, carry):
    n_heads, page = blocks[0].shape
    stacked = jnp.concatenate(blocks, axis=0)
    r = lax.broadcasted_iota(jnp.int32, (page, page), 0)
    c = lax.broadcasted_iota(jnp.int32, (page, page), 1)
    within = _dot_exact_lhs(stacked, jnp.where(r <= c, 1.0, 0.0).astype(BF16))
    sums = []
    for b in range(len(blocks)):
        blk = within[b * n_heads:(b + 1) * n_heads] + carry
        sums.append(blk)
        carry = blk[:, page - 1:page]
    return sums, carry


def _sample_chunk(chunk, last, very_first, q_ref, knew_ref, vnew_ref, lfnew_ref, k_pages, v_pages, f_pages,
                  o_ref, qbd_ref, st_ref, acc_ref, m_ref, l_ref, fc_ref, knew_buf, vnew_buf, lfnew_buf,
                  *, steps, d):
    n_heads = d // HEAD_DIM
    page = k_pages[0].shape[2]
    rows = n_heads * steps
    assert page == LANES and rows % 8 == 0 and 3 * n_heads <= LANES

    row_id = lax.broadcasted_iota(jnp.int32, (rows, 1), 0)
    row_head = _div(row_id, steps)
    row_step = _mod(row_id, steps)
    lane_head = _div(lax.broadcasted_iota(jnp.int32, (1, d), 1), HEAD_DIM)

    @pl.when(very_first)
    def _():
        knew_buf[...] = jnp.zeros_like(knew_buf)
        vnew_buf[...] = jnp.zeros_like(vnew_buf)
        lfnew_buf[...] = jnp.zeros_like(lfnew_buf)

    @pl.when(chunk == 0)
    def _():
        q = q_ref[0]
        qbd = jnp.zeros((rows, d), F32)
        for i in range(steps):
            qbd = jnp.where((lane_head == row_head) & (row_step == i), q[i:i + 1, :], qbd)
        lane = lax.broadcasted_iota(jnp.int32, (1, LANES), 1)
        pick = (lane == row_head) | (lane == n_heads + row_head) | (lane == 2 * n_heads + row_head)
        qbd_ref[...] = jnp.concatenate([qbd, jnp.where(pick, -1.0, 0.0)], axis=1)
        m_ref[...] = jnp.full_like(m_ref, NEG)
        l_ref[...] = jnp.zeros_like(l_ref)
        acc_ref[...] = jnp.zeros_like(acc_ref)
        fc_ref[...] = jnp.zeros_like(fc_ref)

    def online_update(st, value_dot):
        m = m_ref[...]
        m_new = jnp.maximum(m, jnp.max(st, axis=1, keepdims=True))
        alpha = jnp.exp(m - m_new)
        p = jnp.exp(st - m_new)
        l_ref[...] = alpha * l_ref[...] + jnp.sum(p, axis=1, keepdims=True)
        m_ref[...] = m_new
        acc_ref[...] = alpha * acc_ref[...] + value_dot(p)

    def scores():
        sums, carry = _forget_prefix([r[0] for r in f_pages], fc_ref[...])
        fc_ref[...] = carry
        hi, mid, lo = _split3(jnp.concatenate(sums, axis=1))
        keys = hi.shape[1]
        terms = jnp.concatenate([hi.astype(F32), mid.astype(F32), lo.astype(F32),
                                 jnp.zeros((LANES - 3 * n_heads, keys), F32)], axis=0)
        kt = jnp.concatenate([jnp.concatenate([r[0] for r in k_pages], axis=1), terms], axis=0)
        st_ref[...] = _dot(qbd_ref[...], kt)

    def update():
        vt = jnp.concatenate([r[0] for r in v_pages], axis=1)
        online_update(st_ref[...], lambda p: _dot_nt(p, vt))

    def finish():
        @pl.when(chunk == last)
        def _():
            knew_buf[0:steps, :] = knew_ref[0]
            vnew_buf[0:steps, :] = vnew_ref[0]
            lfnew_buf[0:steps, 0:n_heads] = lfnew_ref[0]
            new_sums, _ = _forget_prefix([lfnew_buf[...].T[:n_heads]], fc_ref[...])
            expand = jnp.where(lax.broadcasted_iota(jnp.int32, (rows, n_heads), 1) == row_head,
                               1.0, 0.0).astype(BF16)
            st = _dot_nt(qbd_ref[:, :d], knew_buf[...]) - _dot_exact_rhs(expand, new_sums[0])
            key = lax.broadcasted_iota(jnp.int32, (1, LANES), 1)
            online_update(jnp.where(key <= row_step, st, NEG), lambda p: _dot(p, vnew_buf[...]))

            res = jnp.where(lane_head == row_head, acc_ref[...] / l_ref[...], 0.0)
            gather = jnp.where(lax.broadcasted_iota(jnp.int32, (8, rows), 0) ==
                               _mod(lax.broadcasted_iota(jnp.int32, (8, rows), 1), steps), 1.0, 0.0).astype(BF16)
            o_ref[0] = _dot_exact_rhs(gather, res)[:steps]

    return scores, update, finish


def _attn_kernel(pt_ref, qt_ref, k_ref, f_ref, vt_ref, qs_ref, knew_ref, vnew_ref, lfnew_ref, *rest,
                 tq, n_heads, pages, steps, d):
    del pt_ref
    k_pages, v_pages, f_pages = rest[:pages], rest[pages:2 * pages], rest[2 * pages:3 * pages]
    (o_ref, os_ref, sa_ref, sb_ref, pa_ref, pb_ref,
     qbd_ref, st_ref, acc_ref, m_ref, l_ref, fc_ref, knew_buf, vnew_buf, lfnew_buf) = rest[3 * pages:]
    chunk = pl.program_id(2)
    very_first = (pl.program_id(0) == 0) & (pl.program_id(1) == 0) & (chunk == 0)
    sample_scores, sample_update, sample_finish = _sample_chunk(
        chunk, pl.num_programs(2) - 1, very_first, qs_ref, knew_ref, vnew_ref, lfnew_ref,
        k_pages, v_pages, f_pages, os_ref, qbd_ref, st_ref, acc_ref, m_ref, l_ref, fc_ref,
        knew_buf, vnew_buf, lfnew_buf, steps=steps, d=d)
    _prompt_tile(qt_ref, k_ref, f_ref, vt_ref, o_ref, sa_ref, sb_ref, pa_ref, pb_ref, tq=tq, n_heads=n_heads,
                 early_work=sample_scores, late_work=sample_update)
    sample_finish()


def _attention(qtb, kb, fb16, vtb, page_table, q_s, k_new, v_new, lf_new, cache_k, cache_v, cache_logf,
               *, n_heads, tq=512):
    b, hp, s, _ = kb.shape
    nb, steps, d = q_s.shape
    n_pages = page_table.shape[1]
    n_phys, page, _ = cache_logf.shape
    rows = n_heads * steps
    n_tiles = s // tq
    assert nb == b * hp and n_pages % n_tiles == 0
    pages = n_pages // n_tiles
    cache_k = jnp.transpose(cache_k, (0, 2, 3, 1)).reshape(n_phys, d, page)
    cache_v = jnp.transpose(cache_v, (0, 2, 3, 1)).reshape(n_phys, d, page)
    cache_logf = jnp.transpose(cache_logf, (0, 2, 1))

    def page_map(r):
        return lambda i, p, j, pt: (pt[(i * hp + p) * n_pages + j * pages + r], 0, 0)

    batch = lambda i, p, j, pt: (i * hp + p, 0, 0)
    kv_specs = [pl.BlockSpec((1, d, page), page_map(r)) for r in range(pages)]
    f_specs = [pl.BlockSpec((1, n_heads, page), page_map(r)) for r in range(pages)]
    grid_spec = pltpu.PrefetchScalarGridSpec(
        num_scalar_prefetch=1,
        grid=(b, hp, n_tiles),
        in_specs=[pl.BlockSpec((1, 1, LANES, tq), lambda i, p, j, pt: (i, p, 0, j)),
                  pl.BlockSpec((1, 1, s, LANES), lambda i, p, j, pt: (i, p, 0, 0)),
                  pl.BlockSpec((1, s, LANES), lambda i, p, j, pt: (i, 0, 0)),
                  pl.BlockSpec((1, 1, LANES, s), lambda i, p, j, pt: (i, p, 0, 0)),
                  pl.BlockSpec((1, steps, d), batch), pl.BlockSpec((1, steps, d), batch),
                  pl.BlockSpec((1, steps, d), batch), pl.BlockSpec((1, steps, n_heads), batch)]
                 + kv_specs + kv_specs + f_specs,
        out_specs=[pl.BlockSpec((1, tq, LANES), lambda i, p, j, pt: (i, j, p)),
                   pl.BlockSpec((1, steps, d), batch)],
        scratch_shapes=[pltpu.VMEM((2, tq, tq), F32), pltpu.VMEM((2, tq, tq), F32),
                        pltpu.VMEM((2, tq, tq), BF16), pltpu.VMEM((2, tq, tq), BF16),
                        pltpu.VMEM((rows, d + LANES), F32), pltpu.VMEM((rows, pages * page), F32),
                        pltpu.VMEM((rows, d), F32),
                        pltpu.VMEM((rows, 1), F32), pltpu.VMEM((rows, 1), F32),
                        pltpu.VMEM((n_heads, 1), F32), pltpu.VMEM((LANES, d), F32),
                        pltpu.VMEM((LANES, d), F32), pltpu.VMEM((LANES, LANES), F32)],
    )
    return pl.pallas_call(
        functools.partial(_attn_kernel, tq=tq, n_heads=n_heads, pages=pages, steps=steps, d=d),
        grid_spec=grid_spec,
        out_shape=[jax.ShapeDtypeStruct((b, s, hp * LANES), BF16), jax.ShapeDtypeStruct((nb, steps, d), F32)],
        compiler_params=_params("arbitrary", "arbitrary", "arbitrary"),
        name="attention",
    )(page_table.reshape(-1), qtb, kb, fb16, vtb, q_s, k_new, v_new, lf_new,
      *([cache_k] * pages), *([cache_v] * pages), *([cache_logf] * pages))


def _outproj_kernel(o_ref, sz_ref, w_ref, postg_ref, gate_ref, x_ref, y_ref):
    g = (o_ref[0].astype(F32) * sz_ref[0].astype(F32)).astype(BF16)
    out = _dot(g, w_ref[...])
    y_ref[0] = x_ref[0] + gate_ref[0] * (_rms(out) * postg_ref[...])


def _outproj(o, sz, w, post_g, gate, x, *, tile, name):
    b, s, d = x.shape
    tok = pl.BlockSpec((1, tile, d), lambda i, j: (i, j, 0))
    per_row = gate.shape[1] == s
    mod = pl.BlockSpec((1, tile if per_row else 1, d),
                       (lambda i, j: (i, j, 0)) if per_row else (lambda i, j: (i, 0, 0)))
    return pl.pallas_call(
        _outproj_kernel,
        grid=(b, s // tile),
        in_specs=[tok, tok, _const_spec(w.shape), _const_spec((1, d)), mod, tok],
        out_specs=tok,
        out_shape=jax.ShapeDtypeStruct((b, s, d), F32),
        compiler_params=_params("arbitrary", "arbitrary"),
        name=name,
    )(o, sz, w, post_g, gate, x)


def kernel(x_prompt, x_sample, state_pool, cache_k, cache_v, cache_logf, page_table, c_prompt, c_sample,
           ada_w, ada_b, pre_g, post_g, a_in_w, a_grp_w, a_scale, a_out_w, kv_g, kv_w, f_b, b_in_w, b_out_w):
    bp, seq, d = x_prompt.shape
    bs, steps, _ = x_sample.shape
    n_heads = d // HEAD_DIM
    w_a = a_out_w.shape[1]
    rows_s = bs * steps

    pad = (-(bp + bs)) % 8
    c_all = jnp.concatenate([c_prompt, c_sample, jnp.zeros((pad, d), F32)], axis=0)
    mod = _adaln(c_all, ada_w, ada_b)

    def mods(layer):
        m = mod[layer]
        parts = [m[:, i * d:(i + 1) * d] for i in range(3)]
        prompt = [p[:bp].reshape(bp, 1, d) for p in parts]
        sample = [jnp.repeat(p[bp:bp + bs], steps, axis=0) for p in parts]
        return prompt, sample

    (shift0_p, scale0_p, gate0_p), (shift0_s, scale0_s, gate0_s) = mods(0)
    (shift1_p, scale1_p, gate1_p), (shift1_s, scale1_s, gate1_s) = mods(1)

    in_w = a_in_w[0].astype(BF16)
    grp_w = a_grp_w[0].astype(BF16)
    out_w = a_out_w[0].astype(BF16)
    kv_w_main = kv_w[:, :2 * d].astype(BF16)
    w_f = jnp.pad(kv_w[:, 2 * d:], ((0, 0), (0, LANES - n_heads))).astype(BF16)
    f_b_pad = jnp.pad(f_b, (0, LANES - n_heads)).reshape(1, LANES)
    bin_w = b_in_w[0].astype(BF16)
    bout_w = b_out_w[0].astype(BF16)
    pre0, pre1 = pre_g[0].reshape(1, d), pre_g[1].reshape(1, d)
    post0, post1 = post_g[0].reshape(1, d), post_g[1].reshape(1, d)
    kvg = kv_g.reshape(1, d)
    asc = a_scale[0].reshape(1, w_a)

    hist = jnp.transpose(state_pool[0], (1, 0, 2)).reshape(POOL_BUF * bs, w_a)
    x1_s, u_s = _pool_sample(x_sample.reshape(rows_s, d), shift0_s, scale0_s, gate0_s, pre0, post0,
                             in_w, hist, grp_w, asc, out_w, steps=steps)
    pool_sample = jnp.concatenate([state_pool[:, :, steps:], u_s.reshape(1, bs, steps, w_a)], axis=2)
    x1_s3 = x1_s.reshape(1, rows_s, d)
    k_s, v_s, sz_s, logf_s, q_s = _proj(
        x1_s3, shift1_s[None], scale1_s[None], kvg, pre1, kv_w_main, w_f, f_b_pad, bin_w,
        tile=rows_s, prompt=False)

    x1_p, tail_p = _pool_prompt(x_prompt, shift0_p, scale0_p, gate0_p, pre0, post0, in_w, grp_w, asc, out_w)
    pool_prompt = tail_p[None, :, HALO - POOL_BUF:, :]
    k_p, vt_p, sz_p, kb, vtb, qtb, fb16, lft_p = _proj(
        x1_p, shift1_p, scale1_p, kvg, pre1, kv_w_main, w_f, f_b_pad, bin_w, tile=512, prompt=True)
    logf_p = jnp.transpose(lft_p, (0, 2, 1))
    v_p = jnp.transpose(vt_p.reshape(bp, n_heads, HEAD_DIM, seq), (0, 3, 1, 2))

    o_p, o_s = _attention(qtb, kb, fb16, vtb, page_table, q_s.reshape(bs, steps, d), k_s.reshape(bs, steps, d),
                          v_s.reshape(bs, steps, d), logf_s.reshape(bs, steps, n_heads),
                          cache_k, cache_v, cache_logf, n_heads=n_heads)
    y_prompt = _outproj(o_p, sz_p, bout_w, post1, gate1_p, x1_p, tile=512, name="outproj_prompt")
    y_sample = _outproj(o_s.reshape(1, rows_s, d), sz_s, bout_w, post1, gate1_s[None], x1_s3,
                        tile=rows_s, name="outproj_sample")

    return (y_prompt, y_sample.reshape(bs, steps, d), pool_prompt, pool_sample,
            k_p.reshape(bp, seq, n_heads, HEAD_DIM), v_p, logf_p,
            k_s.reshape(bs, steps, n_heads, HEAD_DIM), v_s.reshape(bs, steps, n_heads, HEAD_DIM),
            logf_s.reshape(bs, steps, n_heads))
```

```python
import functools

import jax
import jax.numpy as jnp
import numpy as np
from jax import lax
from jax.experimental import pallas as pl
from jax.experimental.pallas import tpu as pltpu

F32 = jnp.float32
BF16 = jnp.bfloat16

EPS = 1e-6
NEG = -1e30
LOG2E = 1.4426950408889634
POOL_WINDOWS = (2, 4, 8, 16)
POOL_BUF = max(POOL_WINDOWS) - 1
HEAD_DIM = 64
HALO = 16

V7X_VMEM_BYTES = 64 * 1024 * 1024
VMEM_LIMIT = V7X_VMEM_BYTES - 8 * 1024 * 1024
LANES = 128
BAND = 256
STRIP = 32


def _params(*sem):
    return pltpu.CompilerParams(dimension_semantics=sem, vmem_limit_bytes=VMEM_LIMIT)


def _const_spec(shape):
    nd = len(shape)
    return pl.BlockSpec(shape, lambda *_: (0,) * nd, pipeline_mode=pl.Buffered(1))


def _silu(x):
    return x * jax.nn.sigmoid(x)


def _rms(x):
    return x * lax.rsqrt(jnp.mean(x * x, axis=-1, keepdims=True) + EPS)


def _split3(x):
    a = x.astype(BF16)
    r = x - a.astype(F32)
    b = r.astype(BF16)
    c = (r - b.astype(F32)).astype(BF16)
    return a, b, c


def _log_sigmoid(x):
    return jnp.minimum(x, 0.0) - jnp.log1p(jnp.exp(-jnp.abs(x)))


def _div(x, n):
    return x >> (n.bit_length() - 1) if n & (n - 1) == 0 else x // n


def _mod(x, n):
    return x & (n - 1) if n & (n - 1) == 0 else x % n


def _dot(a, b):
    return jnp.dot(a, b, preferred_element_type=F32)


def _dot_nt(a, b):
    return lax.dot_general(a, b, (((1,), (1,)), ((), ())), preferred_element_type=F32)


def _dot_exact_rhs(sel, x):
    a, b, c = _split3(x)
    return _dot(sel, a) + _dot(sel, b) + _dot(sel, c)


def _dot_exact_lhs(x, sel):
    a, b, c = _split3(x)
    return _dot(a, sel) + _dot(b, sel) + _dot(c, sel)


def _adaln_kernel(c_ref, w_ref, b_ref, o_ref):
    a = _silu(c_ref[...]).astype(BF16)
    o_ref[0] = _dot(a, w_ref[0].astype(BF16)) + b_ref[0]


def _adaln(c_all, ada_w, ada_b):
    depth, d, d3 = ada_w.shape
    rows = c_all.shape[0]
    tn = d3 // 2
    return pl.pallas_call(
        _adaln_kernel,
        grid=(depth, d3 // tn),
        in_specs=[
            pl.BlockSpec((rows, d), lambda l, j: (0, 0)),
            pl.BlockSpec((1, d, tn), lambda l, j: (l, 0, j)),
            pl.BlockSpec((1, 1, tn), lambda l, j: (l, 0, j)),
        ],
        out_specs=pl.BlockSpec((1, rows, tn), lambda l, j: (l, 0, j)),
        out_shape=jax.ShapeDtypeStruct((depth, rows, d3), F32),
        compiler_params=_params("arbitrary", "arbitrary"),
        name="adaln",
    )(c_all, ada_w, ada_b.reshape(depth, 1, d3))


def _band_constants():
    t = np.arange(BAND)[:, None]
    s = np.arange(BAND)[None, :]
    eye = (t == s).astype(np.float32)

    def one(w, first):
        cnt = np.minimum(t + 1, w) if first else w
        return ((t - s >= 1) & (t - s < w)).astype(np.float32) - (cnt - 1) * eye

    band = np.stack([np.stack([one(w, first) for w in POOL_WINDOWS]) for first in (True, False)])
    th = np.arange(HALO)[:, None]
    jh = np.arange(HALO)[None, :]
    bandh = np.stack([(th - jh + HALO < w) for w in POOL_WINDOWS]).astype(np.float32)
    return jnp.asarray(band, BF16), jnp.asarray(bandh, BF16)


def _pool_prompt_kernel(x_ref, shift_ref, scale_ref, gate_ref, preg_ref, postg_ref, inw_ref,
                        band_ref, bandh_ref, grpw_ref, ascale_ref, outw_ref,
                        x1_ref, tail_ref, halo_ref, *, tile, w_a):
    t = pl.program_id(1)
    gw = w_a // len(POOL_WINDOWS)

    @pl.when(t == 0)
    def _():
        halo_ref[...] = jnp.zeros_like(halo_ref)

    x = x_ref[0]
    h = (_rms(x) * preg_ref[...]) * (1.0 + scale_ref[0]) + shift_ref[0]
    uz = _dot(h.astype(BF16), inw_ref[...])
    u = uz[:, :w_a]
    z = uz[:, w_a:]
    u_bf = u.astype(BF16)

    gated_blocks = []
    for blk in range(tile // BAND):
        r0 = blk * BAND
        halo = halo_ref[...] if blk == 0 else u_bf[r0 - HALO:r0]
        pos = t * tile + r0 + lax.broadcasted_iota(jnp.int32, (BAND, 1), 0)
        kind = jnp.where(t == 0, 0, 1) if blk == 0 else 1
        cols = []
        for g, w in enumerate(POOL_WINDOWS):
            c0 = g * gw
            ug = u_bf[r0:r0 + BAND, c0:c0 + gw]
            wsum = _dot(band_ref[kind, g], ug)
            top = wsum[:HALO] + _dot(bandh_ref[g], halo[:, c0:c0 + gw])
            wsum = jnp.concatenate([top, wsum[HALO:]], axis=0)
            cnt = jnp.minimum(pos + 1, w).astype(F32)
            pooled = wsum / cnt
            mixed = _dot(pooled.astype(BF16), grpw_ref[g]) * ascale_ref[:, c0:c0 + gw]
            cols.append((mixed * _silu(z[r0:r0 + BAND, c0:c0 + gw])).astype(BF16))
        gated_blocks.append(jnp.concatenate(cols, axis=1))
    gated = gated_blocks[0] if len(gated_blocks) == 1 else jnp.concatenate(gated_blocks, axis=0)

    y = _dot(gated, outw_ref[...])
    x1_ref[0] = x + gate_ref[0] * (_rms(y) * postg_ref[...])

    halo_ref[...] = u_bf[tile - HALO:]

    @pl.when(t == pl.num_programs(1) - 1)
    def _():
        tail_ref[0] = u[tile - HALO:]


def _pool_prompt(x, shift, scale, gate, pre_g, post_g, in_w, grp_w, a_scale, out_w, *, tile=512):
    b, s, d = x.shape
    w_a = out_w.shape[0]
    band, bandh = _band_constants()
    tok = pl.BlockSpec((1, tile, d), lambda i, j: (i, j, 0))
    mod = pl.BlockSpec((1, 1, d), lambda i, j: (i, 0, 0))
    return pl.pallas_call(
        functools.partial(_pool_prompt_kernel, tile=tile, w_a=w_a),
        grid=(b, s // tile),
        in_specs=[tok, mod, mod, mod, _const_spec((1, d)), _const_spec((1, d)),
                  _const_spec(in_w.shape), _const_spec(band.shape), _const_spec(bandh.shape),
                  _const_spec(grp_w.shape), _const_spec((1, w_a)), _const_spec(out_w.shape)],
        out_specs=[tok, pl.BlockSpec((1, HALO, w_a), lambda i, j: (i, 0, 0))],
        out_shape=[jax.ShapeDtypeStruct((b, s, d), F32),
                   jax.ShapeDtypeStruct((b, HALO, w_a), F32)],
        scratch_shapes=[pltpu.VMEM((HALO, w_a), BF16)],
        compiler_params=_params("arbitrary", "arbitrary"),
        name="pool_prompt",
    )(x, shift, scale, gate, pre_g, post_g, in_w, band, bandh, grp_w, a_scale, out_w)


def _pool_sample_kernel(x_ref, shift_ref, scale_ref, gate_ref, preg_ref, postg_ref, inw_ref,
                        hist_ref, grpw_ref, ascale_ref, outw_ref, x1_ref, u_ref, *, steps, w_a):
    gw = w_a // len(POOL_WINDOWS)
    rows = x_ref.shape[0]
    nh = hist_ref.shape[0]

    x = x_ref[...]
    h = (_rms(x) * preg_ref[...]) * (1.0 + scale_ref[...]) + shift_ref[...]
    uz = _dot(h.astype(BF16), inw_ref[...])
    u = uz[:, :w_a]
    z = uz[:, w_a:]
    u_ref[...] = u
    u_bf = u.astype(BF16)
    hist_bf = hist_ref[...].astype(BF16)

    n_batch = nh // POOL_BUF
    ro = lax.broadcasted_iota(jnp.int32, (rows, rows), 0)
    ri = lax.broadcasted_iota(jnp.int32, (rows, rows), 1)
    same = _div(ro, steps) == _div(ri, steps)
    lag_new = _mod(ro, steps) - _mod(ri, steps)
    roh = lax.broadcasted_iota(jnp.int32, (rows, nh), 0)
    ch = lax.broadcasted_iota(jnp.int32, (rows, nh), 1)
    in_batch = _mod(ch, n_batch) == _div(roh, steps)
    lag_hist = POOL_BUF + _mod(roh, steps) - _div(ch, n_batch)

    cols = []
    for g, w in enumerate(POOL_WINDOWS):
        c0 = g * gw
        sel_new = jnp.where(same & (lag_new >= 1) & (lag_new < w), 1.0, 0.0)
        sel_new = jnp.where(ro == ri, -(w - 1.0), sel_new).astype(BF16)
        sel_hist = jnp.where(in_batch & (lag_hist < w), 1.0, 0.0).astype(BF16)
        wsum = _dot(sel_new, u_bf[:, c0:c0 + gw]) + _dot(sel_hist, hist_bf[:, c0:c0 + gw])
        pooled = wsum / float(w)
        mixed = _dot(pooled.astype(BF16), grpw_ref[g]) * ascale_ref[:, c0:c0 + gw]
        cols.append((mixed * _silu(z[:, c0:c0 + gw])).astype(BF16))
    gated = jnp.concatenate(cols, axis=1)
    y = _dot(gated, outw_ref[...])
    x1_ref[...] = x + gate_ref[...] * (_rms(y) * postg_ref[...])


def _pool_sample(x, shift, scale, gate, pre_g, post_g, in_w, hist, grp_w, a_scale, out_w, *, steps):
    rows, d = x.shape
    w_a = out_w.shape[0]
    args = (x, shift, scale, gate, pre_g, post_g, in_w, hist, grp_w, a_scale, out_w)
    return pl.pallas_call(
        functools.partial(_pool_sample_kernel, steps=steps, w_a=w_a),
        grid=(1,),
        in_specs=[_const_spec(a.shape) for a in args],
        out_specs=[pl.BlockSpec((rows, d), lambda i: (0, 0)),
                   pl.BlockSpec((rows, w_a), lambda i: (0, 0))],
        out_shape=[jax.ShapeDtypeStruct((rows, d), F32),
                   jax.ShapeDtypeStruct((rows, w_a), F32)],
        compiler_params=_params("arbitrary"),
        name="pool_sample",
    )(*args)


def _proj_kernel(x_ref, shift_ref, scale_ref, kvg_ref, preg_ref, kvw_ref, wf_ref, fb_ref, binw_ref,
                 *rest, tile, d, prompt):
    if prompt:
        ltri_ref, place_ref, k_ref, vt_ref, sz_ref, kb_ref, vtb_ref, qtb_ref, fb16_ref, lft_ref, carry_ref = rest
    else:
        k_ref, v_ref, sz_ref, logf_ref, q_ref = rest
    n_heads = d // HEAD_DIM

    r = _rms(x_ref[0])
    xn = (r * kvg_ref[...]).astype(BF16)
    h = ((r * preg_ref[...]) * (1.0 + scale_ref[0]) + shift_ref[0]).astype(BF16)

    kv = _dot(xn, kvw_ref[...])
    k = kv[:, :d]
    v = kv[:, d:]
    k_ref[0] = k
    logf = _log_sigmoid(_dot(xn, wf_ref[...]) + fb_ref[...])

    qz = _dot(h, binw_ref[...])
    sz_ref[0] = _silu(qz[:, d:]).astype(BF16)

    if prompt:
        qt = (qz[:, :d] * (HEAD_DIM ** -0.5 * LOG2E)).T
        vt = v.T
        vt_ref[0] = vt
        for hp in range(d // LANES):
            sl = slice(hp * LANES, (hp + 1) * LANES)
            kb_ref[0, hp] = k[:, sl].astype(BF16)
            vtb_ref[0, hp] = vt[sl].astype(BF16)
            qtb_ref[0, hp] = qt[sl].astype(BF16)

        @pl.when(pl.program_id(1) == 0)
        def _():
            carry_ref[...] = jnp.zeros_like(carry_ref)

        lft_ref[0] = logf.T[:n_heads]
        cum = _dot_exact_rhs(ltri_ref[...], logf) + carry_ref[...]
        carry_ref[...] = cum[tile - 1:tile]
        hi, mid, lo = _split3(cum * LOG2E)
        fb16_ref[0] = (_dot(hi, place_ref[0]) + _dot(mid, place_ref[1]) + _dot(lo, place_ref[2])).astype(BF16)
    else:
        v_ref[0] = v
        logf_ref[0] = logf[:, :n_heads]
        q_ref[0] = qz[:, :d] * (HEAD_DIM ** -0.5)


def _proj(x, shift, scale, kv_g, pre_g, kv_w_main, w_f, f_b, b_in_w, *, tile, prompt):
    b, s, d = x.shape
    n_heads = d // HEAD_DIM
    hp = d // LANES
    tok = pl.BlockSpec((1, tile, d), lambda i, j: (i, j, 0))
    mod_rows = shift.shape[1]
    mod_tile = tile if mod_rows == s else 1
    mod = pl.BlockSpec((1, mod_tile, d), (lambda i, j: (i, j, 0)) if mod_rows == s else (lambda i, j: (i, 0, 0)))
    in_specs = [tok, mod, mod, _const_spec((1, d)), _const_spec((1, d)), _const_spec(kv_w_main.shape),
                _const_spec(w_f.shape), _const_spec(f_b.shape), _const_spec(b_in_w.shape)]
    args = [x, shift, scale, kv_g, pre_g, kv_w_main, w_f, f_b, b_in_w]
    scratch = []
    if prompt:
        ltri = jnp.asarray(np.tril(np.ones((tile, tile), np.float32)), BF16)
        place = np.zeros((3, LANES, LANES), np.float32)
        for term in range(3):
            place[term, np.arange(n_heads), term * n_heads + np.arange(n_heads)] = 1.0
        place = jnp.asarray(place, BF16)
        in_specs += [_const_spec(ltri.shape), _const_spec(place.shape)]
        args += [ltri, place]
        pair = pl.BlockSpec((1, hp, tile, LANES), lambda i, j: (i, 0, j, 0))
        pair_t = pl.BlockSpec((1, hp, LANES, tile), lambda i, j: (i, 0, 0, j))
        out_specs = [tok, pl.BlockSpec((1, d, tile), lambda i, j: (i, 0, j)), tok, pair, pair_t, pair_t,
                     pl.BlockSpec((1, tile, LANES), lambda i, j: (i, j, 0)),
                     pl.BlockSpec((1, n_heads, tile), lambda i, j: (i, 0, j))]
        out_shape = [jax.ShapeDtypeStruct((b, s, d), F32), jax.ShapeDtypeStruct((b, d, s), F32),
                     jax.ShapeDtypeStruct((b, s, d), BF16), jax.ShapeDtypeStruct((b, hp, s, LANES), BF16),
                     jax.ShapeDtypeStruct((b, hp, LANES, s), BF16), jax.ShapeDtypeStruct((b, hp, LANES, s), BF16),
                     jax.ShapeDtypeStruct((b, s, LANES), BF16), jax.ShapeDtypeStruct((b, n_heads, s), F32)]
        scratch = [pltpu.VMEM((1, LANES), F32)]
    else:
        out_specs = [tok, tok, tok, pl.BlockSpec((1, tile, n_heads), lambda i, j: (i, j, 0)), tok]
        out_shape = [jax.ShapeDtypeStruct((b, s, d), F32), jax.ShapeDtypeStruct((b, s, d), F32),
                     jax.ShapeDtypeStruct((b, s, d), BF16), jax.ShapeDtypeStruct((b, s, n_heads), F32),
                     jax.ShapeDtypeStruct((b, s, d), F32)]
    return pl.pallas_call(
        functools.partial(_proj_kernel, tile=tile, d=d, prompt=prompt),
        grid=(b, s // tile),
        in_specs=in_specs,
        out_specs=out_specs,
        out_shape=out_shape,
        scratch_shapes=scratch,
        compiler_params=_params("arbitrary", "arbitrary"),
        name="proj_prompt" if prompt else "proj_sample",
    )(*args)


def _prompt_tile(qt_ref, k_ref, f_ref, vt_ref, o_ref, sa_ref, sb_ref, pa_ref, pb_ref, *, tq, n_heads,
                 early_work, late_work):
    tk = tq
    pair = pl.program_id(1)
    qi = pl.program_id(2)
    qt = qt_ref[0, 0]
    row = lax.broadcasted_iota(jnp.int32, (LANES, 1), 0)
    zero = jnp.zeros_like(qt)
    qat = []
    for hd in range(2):
        head = 2 * pair + hd
        mine = (row >= hd * HEAD_DIM) & (row < (hd + 1) * HEAD_DIM)
        pick = (row == head) | (row == n_heads + head) | (row == 2 * n_heads + head)
        minus = jnp.where(pick, -1.0, 0.0).astype(BF16)
        qat.append(jnp.concatenate([jnp.where(mine, qt, zero), jnp.broadcast_to(minus, (LANES, tq))], axis=0))

    def qk(tile, s_ref):
        k0 = pl.multiple_of(tile * tk, tk)
        ka = jnp.concatenate([k_ref[0, 0, pl.ds(k0, tk), :], f_ref[0, pl.ds(k0, tk), :]], axis=1)
        for hd in range(2):
            s_ref[hd] = _dot(ka, qat[hd])

    half = tq // 2

    def qk_diagonal(tile, s_ref):
        k0 = pl.multiple_of(tile * tk, tk)
        ka = jnp.concatenate([k_ref[0, 0, pl.ds(k0, tk), :], f_ref[0, pl.ds(k0, tk), :]], axis=1)
        for hd in range(2):
            s_ref[hd, :half, :] = _dot(ka[:half], qat[hd])
            s_ref[hd, half:, half:] = _dot(ka[half:], qat[hd][:, half:])

    ones = jnp.ones((8, tk), BF16)

    def pv(tile, p_ref, diagonal=False):
        k0 = pl.multiple_of(tile * tk, tk)
        vt = vt_ref[0, 0, :, pl.ds(k0, tk)]
        parts = []
        for hd in range(2):
            va = jnp.concatenate([vt[hd * HEAD_DIM:(hd + 1) * HEAD_DIM], ones], axis=0)
            if diagonal:
                early = _dot(va[:, :half], p_ref[hd, :half, :])
                late = _dot(va[:, half:], p_ref[hd, half:, half:])
                parts.append(jnp.concatenate([early[:, :half], early[:, half:] + late], axis=1))
            else:
                parts.append(_dot(va, p_ref[hd]))
        return parts

    def softmax(s_ref, p_ref, stats, diagonal=False):
        new_stats, alphas = [], []

        def scores(hd, r0):
            c0 = half if diagonal and r0 >= half else 0
            st = s_ref[hd, r0:r0 + STRIP, c0:]
            if diagonal:
                key = lax.broadcasted_iota(jnp.int32, st.shape, 0) + r0
                qry = lax.broadcasted_iota(jnp.int32, st.shape, 1) + c0
                st = jnp.where(key <= qry, st, NEG)
            return st, c0

        def fold(x, op):
            return op(x.reshape(STRIP // 8, 8, x.shape[1]), axis=0)

        for hd in range(2):
            m = stats[hd]
            top = jnp.full((8, tq), NEG, F32)
            for r0 in range(0, tk, STRIP):
                st, c0 = scores(hd, r0)
                upd = jnp.maximum(top[:, c0:], fold(st, jnp.max))
                top = upd if c0 == 0 else jnp.concatenate([top[:, :c0], upd], axis=1)
            m_new = jnp.maximum(m, jnp.max(top, axis=0, keepdims=True))
            alphas.append(jnp.exp2(m - m_new))
            for r0 in range(0, tk, STRIP):
                st, c0 = scores(hd, r0)
                p_ref[hd, r0:r0 + STRIP, c0:] = jnp.exp2((st - m_new[:, c0:]).astype(BF16))
            new_stats.append(m_new)
        return new_stats, alphas

    def rescale_add(alpha, acc, part):
        return [alpha[hd] * acc[hd] + part[hd] for hd in range(2)]

    def pv_before(a):
        return [jnp.where(a > 0, part, 0.0) for part in pv(jnp.maximum(a - 1, 0), pb_ref)]

    @pl.when((pl.program_id(0) == 0) & (pair == 0) & (qi == 0))
    def _():
        pb_ref[...] = jnp.zeros_like(pb_ref)

    early_work()
    qk(0, sa_ref)
    stats = [jnp.full((1, tq), NEG, F32) for _ in range(2)]
    alpha_b = [jnp.ones((1, tq), F32) for _ in range(2)]
    acc = [jnp.zeros((HEAD_DIM + 8, tq), F32) for _ in range(2)]

    def body(t, carry):
        stats, alpha_b, acc = carry
        a = 2 * t
        part = pv_before(a)
        qk(a + 1, sb_ref)
        stats, alpha_a = softmax(sa_ref, pa_ref, stats)
        acc = rescale_add(alpha_b, acc, part)
        part = pv(a, pa_ref)
        qk(a + 2, sa_ref)
        stats, alpha_b = softmax(sb_ref, pb_ref, stats)
        acc = rescale_add(alpha_a, acc, part)
        return stats, alpha_b, acc

    pairs = qi // 2
    stats, alpha_b, acc = lax.fori_loop(0, pairs, body, (stats, alpha_b, acc))
    a = 2 * pairs

    def finish(acc):
        o_ref[0] = jnp.concatenate(
            [acc[hd][:HEAD_DIM] / acc[hd][HEAD_DIM:HEAD_DIM + 1] for hd in range(2)], axis=0).T.astype(o_ref.dtype)

    @pl.when(a == qi)
    def _():
        part = pv_before(a)
        late_work()
        _, alpha_a = softmax(sa_ref, pa_ref, stats, diagonal=True)
        out = rescale_add(alpha_b, acc, part)
        finish(rescale_add(alpha_a, out, pv(a, pa_ref, diagonal=True)))

    @pl.when(a != qi)
    def _():
        part = pv_before(a)
        qk_diagonal(a + 1, sb_ref)
        late_work()
        mid, alpha_a = softmax(sa_ref, pa_ref, stats)
        out = rescale_add(alpha_b, acc, part)
        part = pv(a, pa_ref)
        _, alpha_d = softmax(sb_ref, pb_ref, mid, diagonal=True)
        out = rescale_add(alpha_a, out, part)
        finish(rescale_add(alpha_d, out, pv(a + 1, pb_ref, diagonal=True)))


def _forget_prefix(blocks, carry):
    n_heads, page = blocks[0].shape
    stacked = jnp.concatenate(blocks, axis=0)
    r = lax.broadcasted_iota(jnp.int32, (page, page), 0)
    c = lax.broadcasted_iota(jnp.int32, (page, page), 1)
    within = _dot_exact_lhs(stacked, jnp.where(r <= c, 1.0, 0.0).astype(BF16))
    sums = []
    for b in range(len(blocks)):
        blk = within[b * n_heads:(b + 1) * n_heads] + carry
        sums.append(blk)
        carry = blk[:, page - 1:page]
    return sums, carry


def _sample_chunk(chunk, last, very_first, q_ref, knew_ref, vnew_ref, lfnew_ref, k_pages, v_pages, f_pages,
                  o_ref, qbd_ref, st_ref, acc_ref, m_ref, l_ref, fc_ref, knew_buf, vnew_buf, lfnew_buf,
                  *, steps, d):
    n_heads = d // HEAD_DIM
    page = k_pages[0].shape[-1]
    rows = n_heads * steps
    assert page == LANES and rows % 8 == 0 and 3 * n_heads <= LANES

    row_id = lax.broadcasted_iota(jnp.int32, (rows, 1), 0)
    row_head = _div(row_id, steps)
    row_step = _mod(row_id, steps)
    lane_head = _div(lax.broadcasted_iota(jnp.int32, (1, d), 1), HEAD_DIM)

    @pl.when(very_first)
    def _():
        knew_buf[...] = jnp.zeros_like(knew_buf)
        vnew_buf[...] = jnp.zeros_like(vnew_buf)
        lfnew_buf[...] = jnp.zeros_like(lfnew_buf)

    @pl.when(chunk == 0)
    def _():
        q = q_ref[0]
        qbd = jnp.zeros((rows, d), F32)
        for i in range(steps):
            qbd = jnp.where((lane_head == row_head) & (row_step == i), q[i:i + 1, :], qbd)
        lane = lax.broadcasted_iota(jnp.int32, (1, LANES), 1)
        pick = (lane == row_head) | (lane == n_heads + row_head) | (lane == 2 * n_heads + row_head)
        qbd_ref[...] = jnp.concatenate([qbd, jnp.where(pick, -1.0, 0.0)], axis=1)
        m_ref[...] = jnp.full_like(m_ref, NEG)
        l_ref[...] = jnp.zeros_like(l_ref)
        acc_ref[...] = jnp.zeros_like(acc_ref)
        fc_ref[...] = jnp.zeros_like(fc_ref)

    def online_update(st, value_dot):
        m = m_ref[...]
        m_new = jnp.maximum(m, jnp.max(st, axis=1, keepdims=True))
        alpha = jnp.exp(m - m_new)
        p = jnp.exp(st - m_new)
        l_ref[...] = alpha * l_ref[...] + jnp.sum(p, axis=1, keepdims=True)
        m_ref[...] = m_new
        acc_ref[...] = alpha * acc_ref[...] + value_dot(p)

    def scores():
        sums, carry = _forget_prefix([r[...] for r in f_pages], fc_ref[...])
        fc_ref[...] = carry
        hi, mid, lo = _split3(jnp.concatenate(sums, axis=1))
        keys = hi.shape[1]
        terms = jnp.concatenate([hi.astype(F32), mid.astype(F32), lo.astype(F32),
                                 jnp.zeros((LANES - 3 * n_heads, keys), F32)], axis=0)
        kt = jnp.concatenate([jnp.concatenate([r[...] for r in k_pages], axis=1), terms], axis=0)
        st_ref[...] = _dot(qbd_ref[...], kt)

    def update():
        vt = jnp.concatenate([r[...] for r in v_pages], axis=1)
        online_update(st_ref[...], lambda p: _dot_nt(p, vt))

    def finish():
        @pl.when(chunk == last)
        def _():
            knew_buf[0:steps, :] = knew_ref[0]
            vnew_buf[0:steps, :] = vnew_ref[0]
            lfnew_buf[0:steps, 0:n_heads] = lfnew_ref[0]
            new_sums, _ = _forget_prefix([lfnew_buf[...].T[:n_heads]], fc_ref[...])
            expand = jnp.where(lax.broadcasted_iota(jnp.int32, (rows, n_heads), 1) == row_head,
                               1.0, 0.0).astype(BF16)
            st = _dot_nt(qbd_ref[:, :d], knew_buf[...]) - _dot_exact_rhs(expand, new_sums[0])
            key = lax.broadcasted_iota(jnp.int32, (1, LANES), 1)
            online_update(jnp.where(key <= row_step, st, NEG), lambda p: _dot(p, vnew_buf[...]))

            res = jnp.where(lane_head == row_head, acc_ref[...] / l_ref[...], 0.0)
            gather = jnp.where(lax.broadcasted_iota(jnp.int32, (8, rows), 0) ==
                               _mod(lax.broadcasted_iota(jnp.int32, (8, rows), 1), steps), 1.0, 0.0).astype(BF16)
            o_ref[0] = _dot_exact_rhs(gather, res)[:steps]

    return scores, update, finish


def _attn_kernel(pt_ref, qt_ref, k_ref, f_ref, vt_ref, qs_ref, knew_ref, vnew_ref, lfnew_ref, *rest,
                 tq, n_heads, pages, steps, d):
    (ck_hbm, cv_hbm, cf_hbm, o_ref, os_ref, sa_ref, sb_ref, pa_ref, pb_ref,
     qbd_ref, st_ref, acc_ref, m_ref, l_ref, fc_ref, knew_buf, vnew_buf, lfnew_buf,
     kbuf, vbuf, fbuf, sems) = rest
    chunk = pl.program_id(2)
    n_chunks = pl.num_programs(2)
    step = (pl.program_id(0) * pl.num_programs(1) + pl.program_id(1)) * n_chunks + chunk
    n_steps = pl.num_programs(0) * pl.num_programs(1) * n_chunks
    slot = lax.rem(step, 2)

    def page_copies(of_step, into):
        copies = []
        for r in range(pages):
            pg = pt_ref[of_step * pages + r]
            copies += [pltpu.make_async_copy(ck_hbm.at[pg], kbuf.at[into, r], sems.at[into, 0]),
                       pltpu.make_async_copy(cv_hbm.at[pg], vbuf.at[into, r], sems.at[into, 1]),
                       pltpu.make_async_copy(cf_hbm.at[pg], fbuf.at[into, r], sems.at[into, 2])]
        return copies

    @pl.when(step == 0)
    def _():
        for c in page_copies(0, 0):
            c.start()

    def page_sync():
        for c in page_copies(step, slot):
            c.wait()
        for c in page_copies(jnp.where(step + 1 < n_steps, step + 1, 0), 1 - slot):
            c.start()

    k_pages = [kbuf.at[slot, r] for r in range(pages)]
    v_pages = [vbuf.at[slot, r] for r in range(pages)]
    f_pages = [fbuf.at[slot, r] for r in range(pages)]
    very_first = step == 0
    sample_scores, sample_update, sample_finish = _sample_chunk(
        chunk, n_chunks - 1, very_first, qs_ref, knew_ref, vnew_ref, lfnew_ref,
        k_pages, v_pages, f_pages, os_ref, qbd_ref, st_ref, acc_ref, m_ref, l_ref, fc_ref,
        knew_buf, vnew_buf, lfnew_buf, steps=steps, d=d)

    def early_work():
        page_sync()
        sample_scores()

    _prompt_tile(qt_ref, k_ref, f_ref, vt_ref, o_ref, sa_ref, sb_ref, pa_ref, pb_ref, tq=tq, n_heads=n_heads,
                 early_work=early_work, late_work=sample_update)
    sample_finish()

    @pl.when(step == n_steps - 1)
    def _():
        for c in page_copies(0, 1 - slot):
            c.wait()


def _attention(qtb, kb, fb16, vtb, page_table, q_s, k_new, v_new, lf_new, cache_k, cache_v, cache_logf,
               *, n_heads, tq=512):
    b, hp, s, _ = kb.shape
    nb, steps, d = q_s.shape
    n_pages = page_table.shape[1]
    n_phys, page, _ = cache_logf.shape
    rows = n_heads * steps
    n_tiles = s // tq
    assert nb == b * hp and n_pages % n_tiles == 0
    pages = n_pages // n_tiles
    cache_k = jnp.transpose(cache_k, (0, 2, 3, 1)).reshape(n_phys, d, page)
    cache_v = jnp.transpose(cache_v, (0, 2, 3, 1)).reshape(n_phys, d, page)
    cache_logf = jnp.transpose(cache_logf, (0, 2, 1))

    batch = lambda i, p, j, pt: (i * hp + p, 0, 0)
    in_hbm = pl.BlockSpec(memory_space=pl.ANY)
    grid_spec = pltpu.PrefetchScalarGridSpec(
        num_scalar_prefetch=1,
        grid=(b, hp, n_tiles),
        in_specs=[pl.BlockSpec((1, 1, LANES, tq), lambda i, p, j, pt: (i, p, 0, j)),
                  pl.BlockSpec((1, 1, s, LANES), lambda i, p, j, pt: (i, p, 0, 0)),
                  pl.BlockSpec((1, s, LANES), lambda i, p, j, pt: (i, 0, 0)),
                  pl.BlockSpec((1, 1, LANES, s), lambda i, p, j, pt: (i, p, 0, 0)),
                  pl.BlockSpec((1, steps, d), batch), pl.BlockSpec((1, steps, d), batch),
                  pl.BlockSpec((1, steps, d), batch), pl.BlockSpec((1, steps, n_heads), batch),
                  in_hbm, in_hbm, in_hbm],
        out_specs=[pl.BlockSpec((1, tq, LANES), lambda i, p, j, pt: (i, j, p)),
                   pl.BlockSpec((1, steps, d), batch)],
        scratch_shapes=[pltpu.VMEM((2, tq, tq), F32), pltpu.VMEM((2, tq, tq), F32),
                        pltpu.VMEM((2, tq, tq), BF16), pltpu.VMEM((2, tq, tq), BF16),
                        pltpu.VMEM((rows, d + LANES), F32), pltpu.VMEM((rows, pages * page), F32),
                        pltpu.VMEM((rows, d), F32),
                        pltpu.VMEM((rows, 1), F32), pltpu.VMEM((rows, 1), F32),
                        pltpu.VMEM((n_heads, 1), F32), pltpu.VMEM((LANES, d), F32),
                        pltpu.VMEM((LANES, d), F32), pltpu.VMEM((LANES, LANES), F32),
                        pltpu.VMEM((2, pages, d, page), F32), pltpu.VMEM((2, pages, d, page), F32),
                        pltpu.VMEM((2, pages, n_heads, page), F32), pltpu.SemaphoreType.DMA((2, 3))],
    )
    return pl.pallas_call(
        functools.partial(_attn_kernel, tq=tq, n_heads=n_heads, pages=pages, steps=steps, d=d),
        grid_spec=grid_spec,
        out_shape=[jax.ShapeDtypeStruct((b, s, hp * LANES), BF16), jax.ShapeDtypeStruct((nb, steps, d), F32)],
        compiler_params=_params("arbitrary", "arbitrary", "arbitrary"),
        name="attention",
    )(page_table.reshape(-1), qtb, kb, fb16, vtb, q_s, k_new, v_new, lf_new, cache_k, cache_v, cache_logf)


def _outproj_kernel(o_ref, sz_ref, w_ref, postg_ref, gate_ref, x_ref, y_ref):
    g = (o_ref[0].astype(F32) * sz_ref[0].astype(F32)).astype(BF16)
    out = _dot(g, w_ref[...])
    y_ref[0] = x_ref[0] + gate_ref[0] * (_rms(out) * postg_ref[...])


def _outproj(o, sz, w, post_g, gate, x, *, tile, name):
    b, s, d = x.shape
    tok = pl.BlockSpec((1, tile, d), lambda i, j: (i, j, 0))
    per_row = gate.shape[1] == s
    mod = pl.BlockSpec((1, tile if per_row else 1, d),
                       (lambda i, j: (i, j, 0)) if per_row else (lambda i, j: (i, 0, 0)))
    return pl.pallas_call(
        _outproj_kernel,
        grid=(b, s // tile),
        in_specs=[tok, tok, _const_spec(w.shape), _const_spec((1, d)), mod, tok],
        out_specs=tok,
        out_shape=jax.ShapeDtypeStruct((b, s, d), F32),
        compiler_params=_params("arbitrary", "arbitrary"),
        name=name,
    )(o, sz, w, post_g, gate, x)


def kernel(x_prompt, x_sample, state_pool, cache_k, cache_v, cache_logf, page_table, c_prompt, c_sample,
           ada_w, ada_b, pre_g, post_g, a_in_w, a_grp_w, a_scale, a_out_w, kv_g, kv_w, f_b, b_in_w, b_out_w):
    bp, seq, d = x_prompt.shape
    bs, steps, _ = x_sample.shape
    n_heads = d // HEAD_DIM
    w_a = a_out_w.shape[1]
    rows_s = bs * steps

    pad = (-(bp + bs)) % 8
    c_all = jnp.concatenate([c_prompt, c_sample, jnp.zeros((pad, d), F32)], axis=0)
    mod = _adaln(c_all, ada_w, ada_b)

    def mods(layer):
        m = mod[layer]
        parts = [m[:, i * d:(i + 1) * d] for i in range(3)]
        prompt = [p[:bp].reshape(bp, 1, d) for p in parts]
        sample = [jnp.repeat(p[bp:bp + bs], steps, axis=0) for p in parts]
        return prompt, sample

    (shift0_p, scale0_p, gate0_p), (shift0_s, scale0_s, gate0_s) = mods(0)
    (shift1_p, scale1_p, gate1_p), (shift1_s, scale1_s, gate1_s) = mods(1)

    in_w = a_in_w[0].astype(BF16)
    grp_w = a_grp_w[0].astype(BF16)
    out_w = a_out_w[0].astype(BF16)
    kv_w_main = kv_w[:, :2 * d].astype(BF16)
    w_f = jnp.pad(kv_w[:, 2 * d:], ((0, 0), (0, LANES - n_heads))).astype(BF16)
    f_b_pad = jnp.pad(f_b, (0, LANES - n_heads)).reshape(1, LANES)
    bin_w = b_in_w[0].astype(BF16)
    bout_w = b_out_w[0].astype(BF16)
    pre0, pre1 = pre_g[0].reshape(1, d), pre_g[1].reshape(1, d)
    post0, post1 = post_g[0].reshape(1, d), post_g[1].reshape(1, d)
    kvg = kv_g.reshape(1, d)
    asc = a_scale[0].reshape(1, w_a)

    hist = jnp.transpose(state_pool[0], (1, 0, 2)).reshape(POOL_BUF * bs, w_a)
    x1_s, u_s = _pool_sample(x_sample.reshape(rows_s, d), shift0_s, scale0_s, gate0_s, pre0, post0,
                             in_w, hist, grp_w, asc, out_w, steps=steps)
    pool_sample = jnp.concatenate([state_pool[:, :, steps:], u_s.reshape(1, bs, steps, w_a)], axis=2)
    x1_s3 = x1_s.reshape(1, rows_s, d)
    k_s, v_s, sz_s, logf_s, q_s = _proj(
        x1_s3, shift1_s[None], scale1_s[None], kvg, pre1, kv_w_main, w_f, f_b_pad, bin_w,
        tile=rows_s, prompt=False)

    x1_p, tail_p = _pool_prompt(x_prompt, shift0_p, scale0_p, gate0_p, pre0, post0, in_w, grp_w, asc, out_w)
    pool_prompt = tail_p[None, :, HALO - POOL_BUF:, :]
    k_p, vt_p, sz_p, kb, vtb, qtb, fb16, lft_p = _proj(
        x1_p, shift1_p, scale1_p, kvg, pre1, kv_w_main, w_f, f_b_pad, bin_w, tile=512, prompt=True)
    logf_p = jnp.transpose(lft_p, (0, 2, 1))
    v_p = jnp.transpose(vt_p.reshape(bp, n_heads, HEAD_DIM, seq), (0, 3, 1, 2))

    o_p, o_s = _attention(qtb, kb, fb16, vtb, page_table, q_s.reshape(bs, steps, d), k_s.reshape(bs, steps, d),
                          v_s.reshape(bs, steps, d), logf_s.reshape(bs, steps, n_heads),
                          cache_k, cache_v, cache_logf, n_heads=n_heads)
    y_prompt = _outproj(o_p, sz_p, bout_w, post1, gate1_p, x1_p, tile=512, name="outproj_prompt")
    y_sample = _outproj(o_s.reshape(1, rows_s, d), sz_s, bout_w, post1, gate1_s[None], x1_s3,
                        tile=rows_s, name="outproj_sample")

    return (y_prompt, y_sample.reshape(bs, steps, d), pool_prompt, pool_sample,
            k_p.reshape(bp, seq, n_heads, HEAD_DIM), v_p, logf_p,
            k_s.reshape(bs, steps, n_heads, HEAD_DIM), v_s.reshape(bs, steps, n_heads, HEAD_DIM),
            logf_s.reshape(bs, steps, n_heads))
```

```python
import functools

import jax
import jax.numpy as jnp
import numpy as np
from jax import lax
from jax.experimental import pallas as pl
from jax.experimental.pallas import tpu as pltpu

F32 = jnp.float32
BF16 = jnp.bfloat16

EPS = 1e-6
NEG = -1e30
LOG2E = 1.4426950408889634
POOL_WINDOWS = (2, 4, 8, 16)
POOL_BUF = max(POOL_WINDOWS) - 1
HEAD_DIM = 64
HALO = 16

V7X_VMEM_BYTES = 64 * 1024 * 1024
VMEM_LIMIT = V7X_VMEM_BYTES - 8 * 1024 * 1024
LANES = 128
BAND = 256
STRIP = 16


def _params(*sem):
    return pltpu.CompilerParams(dimension_semantics=sem, vmem_limit_bytes=VMEM_LIMIT)


def _const_spec(shape):
    nd = len(shape)
    return pl.BlockSpec(shape, lambda *_: (0,) * nd, pipeline_mode=pl.Buffered(1))


def _silu(x):
    return x * jax.nn.sigmoid(x)


def _rms(x):
    return x * lax.rsqrt(jnp.mean(x * x, axis=-1, keepdims=True) + EPS)


def _split3(x):
    a = x.astype(BF16)
    r = x - a.astype(F32)
    b = r.astype(BF16)
    c = (r - b.astype(F32)).astype(BF16)
    return a, b, c


def _log_sigmoid(x):
    return jnp.minimum(x, 0.0) - jnp.log1p(jnp.exp(-jnp.abs(x)))


def _div(x, n):
    return x >> (n.bit_length() - 1) if n & (n - 1) == 0 else x // n


def _mod(x, n):
    return x & (n - 1) if n & (n - 1) == 0 else x % n


def _dot(a, b):
    return jnp.dot(a, b, preferred_element_type=F32)


def _dot_nt(a, b):
    return lax.dot_general(a, b, (((1,), (1,)), ((), ())), preferred_element_type=F32)


def _dot_exact_rhs(sel, x):
    a, b, c = _split3(x)
    return _dot(sel, a) + _dot(sel, b) + _dot(sel, c)


def _dot_exact_lhs(x, sel):
    a, b, c = _split3(x)
    return _dot(a, sel) + _dot(b, sel) + _dot(c, sel)


def _adaln_kernel(c_ref, w_ref, b_ref, o_ref):
    a = _silu(c_ref[...]).astype(BF16)
    o_ref[0] = _dot(a, w_ref[0].astype(BF16)) + b_ref[0]


def _adaln(c_all, ada_w, ada_b):
    depth, d, d3 = ada_w.shape
    rows = c_all.shape[0]
    tn = d3 // 2
    return pl.pallas_call(
        _adaln_kernel,
        grid=(depth, d3 // tn),
        in_specs=[
            pl.BlockSpec((rows, d), lambda l, j: (0, 0)),
            pl.BlockSpec((1, d, tn), lambda l, j: (l, 0, j)),
            pl.BlockSpec((1, 1, tn), lambda l, j: (l, 0, j)),
        ],
        out_specs=pl.BlockSpec((1, rows, tn), lambda l, j: (l, 0, j)),
        out_shape=jax.ShapeDtypeStruct((depth, rows, d3), F32),
        compiler_params=_params("arbitrary", "arbitrary"),
        name="adaln",
    )(c_all, ada_w, ada_b.reshape(depth, 1, d3))


def _band_constants():
    t = np.arange(BAND)[:, None]
    s = np.arange(BAND)[None, :]
    eye = (t == s).astype(np.float32)

    def one(w, first):
        cnt = np.minimum(t + 1, w) if first else w
        return ((t - s >= 1) & (t - s < w)).astype(np.float32) - (cnt - 1) * eye

    band = np.stack([np.stack([one(w, first) for w in POOL_WINDOWS]) for first in (True, False)])
    th = np.arange(HALO)[:, None]
    jh = np.arange(HALO)[None, :]
    bandh = np.stack([(th - jh + HALO < w) for w in POOL_WINDOWS]).astype(np.float32)
    return jnp.asarray(band, BF16), jnp.asarray(bandh, BF16)


def _pool_prompt_kernel(x_ref, shift_ref, scale_ref, gate_ref, preg_ref, postg_ref, inw_ref,
                        band_ref, bandh_ref, grpw_ref, ascale_ref, outw_ref,
                        x1_ref, tail_ref, halo_ref, *, tile, w_a):
    t = pl.program_id(1)
    gw = w_a // len(POOL_WINDOWS)

    @pl.when(t == 0)
    def _():
        halo_ref[...] = jnp.zeros_like(halo_ref)

    x = x_ref[0]
    h = (_rms(x) * preg_ref[...]) * (1.0 + scale_ref[0]) + shift_ref[0]
    uz = _dot(h.astype(BF16), inw_ref[...])
    u = uz[:, :w_a]
    z = uz[:, w_a:]
    u_bf = u.astype(BF16)

    gated_blocks = []
    for blk in range(tile // BAND):
        r0 = blk * BAND
        halo = halo_ref[...] if blk == 0 else u_bf[r0 - HALO:r0]
        pos = t * tile + r0 + lax.broadcasted_iota(jnp.int32, (BAND, 1), 0)
        kind = jnp.where(t == 0, 0, 1) if blk == 0 else 1
        cols = []
        for g, w in enumerate(POOL_WINDOWS):
            c0 = g * gw
            ug = u_bf[r0:r0 + BAND, c0:c0 + gw]
            wsum = _dot(band_ref[kind, g], ug)
            top = wsum[:HALO] + _dot(bandh_ref[g], halo[:, c0:c0 + gw])
            wsum = jnp.concatenate([top, wsum[HALO:]], axis=0)
            cnt = jnp.minimum(pos + 1, w).astype(F32)
            pooled = wsum / cnt
            mixed = _dot(pooled.astype(BF16), grpw_ref[g]) * ascale_ref[:, c0:c0 + gw]
            cols.append((mixed * _silu(z[r0:r0 + BAND, c0:c0 + gw])).astype(BF16))
        gated_blocks.append(jnp.concatenate(cols, axis=1))
    gated = gated_blocks[0] if len(gated_blocks) == 1 else jnp.concatenate(gated_blocks, axis=0)

    y = _dot(gated, outw_ref[...])
    x1_ref[0] = x + gate_ref[0] * (_rms(y) * postg_ref[...])

    halo_ref[...] = u_bf[tile - HALO:]

    @pl.when(t == pl.num_programs(1) - 1)
    def _():
        tail_ref[0] = u[tile - HALO:]


def _pool_prompt(x, shift, scale, gate, pre_g, post_g, in_w, grp_w, a_scale, out_w, *, tile=512):
    b, s, d = x.shape
    w_a = out_w.shape[0]
    band, bandh = _band_constants()
    tok = pl.BlockSpec((1, tile, d), lambda i, j: (i, j, 0))
    mod = pl.BlockSpec((1, 1, d), lambda i, j: (i, 0, 0))
    return pl.pallas_call(
        functools.partial(_pool_prompt_kernel, tile=tile, w_a=w_a),
        grid=(b, s // tile),
        in_specs=[tok, mod, mod, mod, _const_spec((1, d)), _const_spec((1, d)),
                  _const_spec(in_w.shape), _const_spec(band.shape), _const_spec(bandh.shape),
                  _const_spec(grp_w.shape), _const_spec((1, w_a)), _const_spec(out_w.shape)],
        out_specs=[tok, pl.BlockSpec((1, HALO, w_a), lambda i, j: (i, 0, 0))],
        out_shape=[jax.ShapeDtypeStruct((b, s, d), F32),
                   jax.ShapeDtypeStruct((b, HALO, w_a), F32)],
        scratch_shapes=[pltpu.VMEM((HALO, w_a), BF16)],
        compiler_params=_params("arbitrary", "arbitrary"),
        name="pool_prompt",
    )(x, shift, scale, gate, pre_g, post_g, in_w, band, bandh, grp_w, a_scale, out_w)


def _pool_sample_kernel(x_ref, shift_ref, scale_ref, gate_ref, preg_ref, postg_ref, inw_ref,
                        hist_ref, grpw_ref, ascale_ref, outw_ref, x1_ref, u_ref, *, steps, w_a):
    gw = w_a // len(POOL_WINDOWS)
    rows = x_ref.shape[0]
    nh = hist_ref.shape[0]

    x = x_ref[...]
    h = (_rms(x) * preg_ref[...]) * (1.0 + scale_ref[...]) + shift_ref[...]
    uz = _dot(h.astype(BF16), inw_ref[...])
    u = uz[:, :w_a]
    z = uz[:, w_a:]
    u_ref[...] = u
    u_bf = u.astype(BF16)
    hist_bf = hist_ref[...].astype(BF16)

    n_batch = nh // POOL_BUF
    ro = lax.broadcasted_iota(jnp.int32, (rows, rows), 0)
    ri = lax.broadcasted_iota(jnp.int32, (rows, rows), 1)
    same = _div(ro, steps) == _div(ri, steps)
    lag_new = _mod(ro, steps) - _mod(ri, steps)
    roh = lax.broadcasted_iota(jnp.int32, (rows, nh), 0)
    ch = lax.broadcasted_iota(jnp.int32, (rows, nh), 1)
    in_batch = _mod(ch, n_batch) == _div(roh, steps)
    lag_hist = POOL_BUF + _mod(roh, steps) - _div(ch, n_batch)

    cols = []
    for g, w in enumerate(POOL_WINDOWS):
        c0 = g * gw
        sel_new = jnp.where(same & (lag_new >= 1) & (lag_new < w), 1.0, 0.0)
        sel_new = jnp.where(ro == ri, -(w - 1.0), sel_new).astype(BF16)
        sel_hist = jnp.where(in_batch & (lag_hist < w), 1.0, 0.0).astype(BF16)
        wsum = _dot(sel_new, u_bf[:, c0:c0 + gw]) + _dot(sel_hist, hist_bf[:, c0:c0 + gw])
        pooled = wsum / float(w)
        mixed = _dot(pooled.astype(BF16), grpw_ref[g]) * ascale_ref[:, c0:c0 + gw]
        cols.append((mixed * _silu(z[:, c0:c0 + gw])).astype(BF16))
    gated = jnp.concatenate(cols, axis=1)
    y = _dot(gated, outw_ref[...])
    x1_ref[...] = x + gate_ref[...] * (_rms(y) * postg_ref[...])


def _pool_sample(x, shift, scale, gate, pre_g, post_g, in_w, hist, grp_w, a_scale, out_w, *, steps):
    rows, d = x.shape
    w_a = out_w.shape[0]
    args = (x, shift, scale, gate, pre_g, post_g, in_w, hist, grp_w, a_scale, out_w)
    return pl.pallas_call(
        functools.partial(_pool_sample_kernel, steps=steps, w_a=w_a),
        grid=(1,),
        in_specs=[_const_spec(a.shape) for a in args],
        out_specs=[pl.BlockSpec((rows, d), lambda i: (0, 0)),
                   pl.BlockSpec((rows, w_a), lambda i: (0, 0))],
        out_shape=[jax.ShapeDtypeStruct((rows, d), F32),
                   jax.ShapeDtypeStruct((rows, w_a), F32)],
        compiler_params=_params("arbitrary"),
        name="pool_sample",
    )(*args)


def _proj_kernel(x_ref, shift_ref, scale_ref, kvg_ref, preg_ref, kvw_ref, wf_ref, fb_ref, binw_ref,
                 *rest, tile, d, prompt):
    if prompt:
        utri_ref, k_ref, vt_ref, sz_ref, kb_ref, vtb_ref, qtb_ref, fb16_ref, lft_ref, carry_ref = rest
    else:
        k_ref, v_ref, sz_ref, logf_ref, q_ref = rest
    n_heads = d // HEAD_DIM

    r = _rms(x_ref[0])
    xn = (r * kvg_ref[...]).astype(BF16)
    h = ((r * preg_ref[...]) * (1.0 + scale_ref[0]) + shift_ref[0]).astype(BF16)

    kv = _dot(xn, kvw_ref[...])
    k = kv[:, :d]
    v = kv[:, d:]
    k_ref[0] = k
    logf = _log_sigmoid(_dot(xn, wf_ref[...]) + fb_ref[...])

    qz = _dot(h, binw_ref[...])
    sz_ref[0] = _silu(qz[:, d:]).astype(BF16)

    if prompt:
        qt = (qz[:, :d] * (HEAD_DIM ** -0.5 * LOG2E)).T
        vt = v.T
        vt_ref[0] = vt
        for hp in range(d // LANES):
            sl = slice(hp * LANES, (hp + 1) * LANES)
            kb_ref[0, hp] = k[:, sl].astype(BF16)
            vtb_ref[0, hp] = vt[sl].astype(BF16)
            qtb_ref[0, hp] = qt[sl].astype(BF16)

        @pl.when(pl.program_id(1) == 0)
        def _():
            carry_ref[...] = jnp.zeros_like(carry_ref)

        lft = logf.T[:n_heads]
        lft_ref[0] = lft
        parts = _dot(jnp.concatenate(_split3(lft), axis=0), utri_ref[...])
        cum = parts[:n_heads] + parts[n_heads:2 * n_heads] + parts[2 * n_heads:] + carry_ref[...]
        carry_ref[...] = cum[:, tile - 1:tile]
        terms = jnp.concatenate([t.astype(F32) for t in _split3(cum * LOG2E)]
                                + [jnp.zeros((LANES - 3 * n_heads, tile), F32)], axis=0)
        fb16_ref[0] = terms.T.astype(BF16)
    else:
        v_ref[0] = v
        logf_ref[0] = logf[:, :n_heads]
        q_ref[0] = qz[:, :d] * (HEAD_DIM ** -0.5)


def _proj(x, shift, scale, kv_g, pre_g, kv_w_main, w_f, f_b, b_in_w, *, tile, prompt):
    b, s, d = x.shape
    n_heads = d // HEAD_DIM
    hp = d // LANES
    tok = pl.BlockSpec((1, tile, d), lambda i, j: (i, j, 0))
    mod_rows = shift.shape[1]
    mod_tile = tile if mod_rows == s else 1
    mod = pl.BlockSpec((1, mod_tile, d), (lambda i, j: (i, j, 0)) if mod_rows == s else (lambda i, j: (i, 0, 0)))
    in_specs = [tok, mod, mod, _const_spec((1, d)), _const_spec((1, d)), _const_spec(kv_w_main.shape),
                _const_spec(w_f.shape), _const_spec(f_b.shape), _const_spec(b_in_w.shape)]
    args = [x, shift, scale, kv_g, pre_g, kv_w_main, w_f, f_b, b_in_w]
    scratch = []
    if prompt:
        utri = jnp.asarray(np.triu(np.ones((tile, tile), np.float32)), BF16)
        in_specs.append(_const_spec(utri.shape))
        args.append(utri)
        pair = pl.BlockSpec((1, hp, tile, LANES), lambda i, j: (i, 0, j, 0))
        pair_t = pl.BlockSpec((1, hp, LANES, tile), lambda i, j: (i, 0, 0, j))
        out_specs = [tok, pl.BlockSpec((1, d, tile), lambda i, j: (i, 0, j)), tok, pair, pair_t, pair_t,
                     pl.BlockSpec((1, tile, LANES), lambda i, j: (i, j, 0)),
                     pl.BlockSpec((1, n_heads, tile), lambda i, j: (i, 0, j))]
        out_shape = [jax.ShapeDtypeStruct((b, s, d), F32), jax.ShapeDtypeStruct((b, d, s), F32),
                     jax.ShapeDtypeStruct((b, s, d), BF16), jax.ShapeDtypeStruct((b, hp, s, LANES), BF16),
                     jax.ShapeDtypeStruct((b, hp, LANES, s), BF16), jax.ShapeDtypeStruct((b, hp, LANES, s), BF16),
                     jax.ShapeDtypeStruct((b, s, LANES), BF16), jax.ShapeDtypeStruct((b, n_heads, s), F32)]
        scratch = [pltpu.VMEM((n_heads, 1), F32)]
    else:
        out_specs = [tok, tok, tok, pl.BlockSpec((1, tile, n_heads), lambda i, j: (i, j, 0)), tok]
        out_shape = [jax.ShapeDtypeStruct((b, s, d), F32), jax.ShapeDtypeStruct((b, s, d), F32),
                     jax.ShapeDtypeStruct((b, s, d), BF16), jax.ShapeDtypeStruct((b, s, n_heads), F32),
                     jax.ShapeDtypeStruct((b, s, d), F32)]
    return pl.pallas_call(
        functools.partial(_proj_kernel, tile=tile, d=d, prompt=prompt),
        grid=(b, s // tile),
        in_specs=in_specs,
        out_specs=out_specs,
        out_shape=out_shape,
        scratch_shapes=scratch,
        compiler_params=_params("arbitrary", "arbitrary"),
        name="proj_prompt" if prompt else "proj_sample",
    )(*args)


def _prompt_tile(qt_ref, k_ref, f_ref, vt_ref, o_ref, sa_ref, sb_ref, pa_ref, pb_ref, *, tq, n_heads,
                 early_work, late_work):
    tk = tq
    pair = pl.program_id(1)
    qi = pl.program_id(2)
    qt = qt_ref[0, 0]
    row = lax.broadcasted_iota(jnp.int32, (LANES, 1), 0)
    zero = jnp.zeros_like(qt)
    qat = []
    for hd in range(2):
        head = 2 * pair + hd
        mine = (row >= hd * HEAD_DIM) & (row < (hd + 1) * HEAD_DIM)
        pick = (row == head) | (row == n_heads + head) | (row == 2 * n_heads + head)
        minus = jnp.where(pick, -1.0, 0.0).astype(BF16)
        qat.append(jnp.concatenate([jnp.where(mine, qt, zero), jnp.broadcast_to(minus, (LANES, tq))], axis=0))

    def qk(tile, s_ref):
        k0 = pl.multiple_of(tile * tk, tk)
        ka = jnp.concatenate([k_ref[0, 0, pl.ds(k0, tk), :], f_ref[0, pl.ds(k0, tk), :]], axis=1)
        for hd in range(2):
            s_ref[hd] = _dot(ka, qat[hd])

    half = tq // 2

    def qk_diagonal(tile, s_ref):
        k0 = pl.multiple_of(tile * tk, tk)
        ka = jnp.concatenate([k_ref[0, 0, pl.ds(k0, tk), :], f_ref[0, pl.ds(k0, tk), :]], axis=1)
        for hd in range(2):
            s_ref[hd, :half, :] = _dot(ka[:half], qat[hd])
            s_ref[hd, half:, half:] = _dot(ka[half:], qat[hd][:, half:])

    ones = jnp.ones((8, tk), BF16)

    def pv(tile, p_ref, diagonal=False):
        k0 = pl.multiple_of(tile * tk, tk)
        vt = vt_ref[0, 0, :, pl.ds(k0, tk)]
        parts = []
        for hd in range(2):
            va = jnp.concatenate([vt[hd * HEAD_DIM:(hd + 1) * HEAD_DIM], ones], axis=0)
            if diagonal:
                early = _dot(va[:, :half], p_ref[hd, :half, :])
                late = _dot(va[:, half:], p_ref[hd, half:, half:])
                parts.append(jnp.concatenate([early[:, :half], early[:, half:] + late], axis=1))
            else:
                parts.append(_dot(va, p_ref[hd]))
        return parts

    def softmax(s_ref, p_ref, stats, diagonal=False):
        new_stats, alphas = [], []

        def scores(hd, r0):
            c0 = half if diagonal and r0 >= half else 0
            st = s_ref[hd, r0:r0 + STRIP, c0:]
            if diagonal:
                key = lax.broadcasted_iota(jnp.int32, st.shape, 0) + r0
                qry = lax.broadcasted_iota(jnp.int32, st.shape, 1) + c0
                st = jnp.where(key <= qry, st, NEG)
            return st, c0

        def fold(x, op):
            return op(x.reshape(STRIP // 8, 8, x.shape[1]), axis=0)

        for hd in range(2):
            m = stats[hd]
            top = jnp.full((8, tq), NEG, F32)
            for r0 in range(0, tk, STRIP):
                st, c0 = scores(hd, r0)
                upd = jnp.maximum(top[:, c0:], fold(st, jnp.max))
                top = upd if c0 == 0 else jnp.concatenate([top[:, :c0], upd], axis=1)
            m_new = jnp.maximum(m, jnp.max(top, axis=0, keepdims=True))
            alphas.append(jnp.exp2(m - m_new))
            for r0 in range(0, tk, STRIP):
                st, c0 = scores(hd, r0)
                p_ref[hd, r0:r0 + STRIP, c0:] = jnp.exp2((st - m_new[:, c0:]).astype(BF16))
            new_stats.append(m_new)
        return new_stats, alphas

    def rescale_add(alpha, acc, part):
        return [alpha[hd] * acc[hd] + part[hd] for hd in range(2)]

    def pv_before(a):
        return [jnp.where(a > 0, part, 0.0) for part in pv(jnp.maximum(a - 1, 0), pb_ref)]

    @pl.when((pl.program_id(0) == 0) & (pair == 0) & (qi == 0))
    def _():
        pb_ref[...] = jnp.zeros_like(pb_ref)

    early_work()
    qk(0, sa_ref)
    stats = [jnp.full((1, tq), NEG, F32) for _ in range(2)]
    alpha_b = [jnp.ones((1, tq), F32) for _ in range(2)]
    acc = [jnp.zeros((HEAD_DIM + 8, tq), F32) for _ in range(2)]

    def body(t, carry):
        stats, alpha_b, acc = carry
        a = 2 * t
        part = pv_before(a)
        qk(a + 1, sb_ref)
        stats, alpha_a = softmax(sa_ref, pa_ref, stats)
        acc = rescale_add(alpha_b, acc, part)
        part = pv(a, pa_ref)
        qk(a + 2, sa_ref)
        stats, alpha_b = softmax(sb_ref, pb_ref, stats)
        acc = rescale_add(alpha_a, acc, part)
        return stats, alpha_b, acc

    pairs = qi // 2
    stats, alpha_b, acc = lax.fori_loop(0, pairs, body, (stats, alpha_b, acc))
    a = 2 * pairs

    def finish(acc):
        o_ref[0] = jnp.concatenate(
            [acc[hd][:HEAD_DIM] / acc[hd][HEAD_DIM:HEAD_DIM + 1] for hd in range(2)], axis=0).T.astype(o_ref.dtype)

    @pl.when(a == qi)
    def _():
        part = pv_before(a)
        late_work()
        _, alpha_a = softmax(sa_ref, pa_ref, stats, diagonal=True)
        out = rescale_add(alpha_b, acc, part)
        finish(rescale_add(alpha_a, out, pv(a, pa_ref, diagonal=True)))

    @pl.when(a != qi)
    def _():
        part = pv_before(a)
        qk_diagonal(a + 1, sb_ref)
        late_work()
        mid, alpha_a = softmax(sa_ref, pa_ref, stats)
        out = rescale_add(alpha_b, acc, part)
        part = pv(a, pa_ref)
        _, alpha_d = softmax(sb_ref, pb_ref, mid, diagonal=True)
        out = rescale_add(alpha_a, out, part)
        finish(rescale_add(alpha_d, out, pv(a + 1, pb_ref, diagonal=True)))


def _forget_prefix(blocks, carry):
    n_heads, page = blocks[0].shape
    stacked = jnp.concatenate(blocks, axis=0)
    r = lax.broadcasted_iota(jnp.int32, (page, page), 0)
    c = lax.broadcasted_iota(jnp.int32, (page, page), 1)
    within = _dot_exact_lhs(stacked, jnp.where(r <= c, 1.0, 0.0).astype(BF16))
    sums = []
    for b in range(len(blocks)):
        blk = within[b * n_heads:(b + 1) * n_heads] + carry
        sums.append(blk)
        carry = blk[:, page - 1:page]
    return sums, carry


def _sample_chunk(chunk, last, very_first, q_ref, knew_ref, vnew_ref, lfnew_ref, k_pages, v_pages, f_pages,
                  o_ref, qbd_ref, st_ref, acc_ref, m_ref, l_ref, fc_ref, knew_buf, vnew_buf, lfnew_buf,
                  *, steps, d):
    n_heads = d // HEAD_DIM
    page = k_pages[0].shape[-1]
    rows = n_heads * steps
    assert page == LANES and rows % 8 == 0 and 3 * n_heads <= LANES

    row_id = lax.broadcasted_iota(jnp.int32, (rows, 1), 0)
    row_head = _div(row_id, steps)
    row_step = _mod(row_id, steps)
    lane_head = _div(lax.broadcasted_iota(jnp.int32, (1, d), 1), HEAD_DIM)

    @pl.when(very_first)
    def _():
        knew_buf[...] = jnp.zeros_like(knew_buf)
        vnew_buf[...] = jnp.zeros_like(vnew_buf)
        lfnew_buf[...] = jnp.zeros_like(lfnew_buf)

    @pl.when(chunk == 0)
    def _():
        q = q_ref[0]
        qbd = jnp.zeros((rows, d), F32)
        for i in range(steps):
            qbd = jnp.where((lane_head == row_head) & (row_step == i), q[i:i + 1, :], qbd)
        lane = lax.broadcasted_iota(jnp.int32, (1, LANES), 1)
        pick = (lane == row_head) | (lane == n_heads + row_head) | (lane == 2 * n_heads + row_head)
        qbd_ref[...] = jnp.concatenate([qbd, jnp.where(pick, -1.0, 0.0)], axis=1)
        m_ref[...] = jnp.full_like(m_ref, NEG)
        l_ref[...] = jnp.zeros_like(l_ref)
        acc_ref[...] = jnp.zeros_like(acc_ref)
        fc_ref[...] = jnp.zeros_like(fc_ref)

    def online_update(st, value_dot):
        m = m_ref[...]
        m_new = jnp.maximum(m, jnp.max(st, axis=1, keepdims=True))
        alpha = jnp.exp(m - m_new)
        p = jnp.exp(st - m_new)
        l_ref[...] = alpha * l_ref[...] + jnp.sum(p, axis=1, keepdims=True)
        m_ref[...] = m_new
        acc_ref[...] = alpha * acc_ref[...] + value_dot(p)

    def scores():
        sums, carry = _forget_prefix([r[...] for r in f_pages], fc_ref[...])
        fc_ref[...] = carry
        hi, mid, lo = _split3(jnp.concatenate(sums, axis=1))
        keys = hi.shape[1]
        terms = jnp.concatenate([hi.astype(F32), mid.astype(F32), lo.astype(F32),
                                 jnp.zeros((LANES - 3 * n_heads, keys), F32)], axis=0)
        kt = jnp.concatenate([jnp.concatenate([r[...] for r in k_pages], axis=1), terms], axis=0)
        st_ref[...] = _dot(qbd_ref[...], kt)

    def update():
        vt = jnp.concatenate([r[...] for r in v_pages], axis=1)
        online_update(st_ref[...], lambda p: _dot_nt(p, vt))

    def finish():
        @pl.when(chunk == last)
        def _():
            knew_buf[0:steps, :] = knew_ref[0]
            vnew_buf[0:steps, :] = vnew_ref[0]
            lfnew_buf[0:steps, 0:n_heads] = lfnew_ref[0]
            new_sums, _ = _forget_prefix([lfnew_buf[...].T[:n_heads]], fc_ref[...])
            expand = jnp.where(lax.broadcasted_iota(jnp.int32, (rows, n_heads), 1) == row_head,
                               1.0, 0.0).astype(BF16)
            st = _dot_nt(qbd_ref[:, :d], knew_buf[...]) - _dot_exact_rhs(expand, new_sums[0])
            key = lax.broadcasted_iota(jnp.int32, (1, LANES), 1)
            online_update(jnp.where(key <= row_step, st, NEG), lambda p: _dot(p, vnew_buf[...]))

            res = jnp.where(lane_head == row_head, acc_ref[...] / l_ref[...], 0.0)
            gather = jnp.where(lax.broadcasted_iota(jnp.int32, (8, rows), 0) ==
                               _mod(lax.broadcasted_iota(jnp.int32, (8, rows), 1), steps), 1.0, 0.0).astype(BF16)
            o_ref[0] = _dot_exact_rhs(gather, res)[:steps]

    return scores, update, finish


def _attn_kernel(pt_ref, qt_ref, k_ref, f_ref, vt_ref, qs_ref, knew_ref, vnew_ref, lfnew_ref, *rest,
                 tq, n_heads, pages, steps, d):
    (ck_hbm, cv_hbm, cf_hbm, o_ref, os_ref, sa_ref, sb_ref, pa_ref, pb_ref,
     qbd_ref, st_ref, acc_ref, m_ref, l_ref, fc_ref, knew_buf, vnew_buf, lfnew_buf,
     kbuf, vbuf, fbuf, sems) = rest
    chunk = pl.program_id(2)
    n_chunks = pl.num_programs(2)
    step = (pl.program_id(0) * pl.num_programs(1) + pl.program_id(1)) * n_chunks + chunk
    n_steps = pl.num_programs(0) * pl.num_programs(1) * n_chunks
    slot = lax.rem(step, 2)

    def page_copies(of_step, into):
        copies = []
        for r in range(pages):
            pg = pt_ref[of_step * pages + r]
            copies += [pltpu.make_async_copy(ck_hbm.at[pg], kbuf.at[into, r], sems.at[into, 0]),
                       pltpu.make_async_copy(cv_hbm.at[pg], vbuf.at[into, r], sems.at[into, 1]),
                       pltpu.make_async_copy(cf_hbm.at[pg], fbuf.at[into, r], sems.at[into, 2])]
        return copies

    @pl.when(step == 0)
    def _():
        for c in page_copies(0, 0):
            c.start()

    def page_sync():
        for c in page_copies(step, slot):
            c.wait()
        for c in page_copies(jnp.where(step + 1 < n_steps, step + 1, 0), 1 - slot):
            c.start()

    k_pages = [kbuf.at[slot, r] for r in range(pages)]
    v_pages = [vbuf.at[slot, r] for r in range(pages)]
    f_pages = [fbuf.at[slot, r] for r in range(pages)]
    very_first = step == 0
    sample_scores, sample_update, sample_finish = _sample_chunk(
        chunk, n_chunks - 1, very_first, qs_ref, knew_ref, vnew_ref, lfnew_ref,
        k_pages, v_pages, f_pages, os_ref, qbd_ref, st_ref, acc_ref, m_ref, l_ref, fc_ref,
        knew_buf, vnew_buf, lfnew_buf, steps=steps, d=d)

    def early_work():
        page_sync()
        sample_scores()

    _prompt_tile(qt_ref, k_ref, f_ref, vt_ref, o_ref, sa_ref, sb_ref, pa_ref, pb_ref, tq=tq, n_heads=n_heads,
                 early_work=early_work, late_work=sample_update)
    sample_finish()

    @pl.when(step == n_steps - 1)
    def _():
        for c in page_copies(0, 1 - slot):
            c.wait()


def _attention(qtb, kb, fb16, vtb, page_table, q_s, k_new, v_new, lf_new, cache_k, cache_v, cache_logf,
               *, n_heads, tq=512):
    b, hp, s, _ = kb.shape
    nb, steps, d = q_s.shape
    n_pages = page_table.shape[1]
    n_phys, page, _ = cache_logf.shape
    rows = n_heads * steps
    n_tiles = s // tq
    assert nb == b * hp and n_pages % n_tiles == 0
    pages = n_pages // n_tiles
    cache_k = jnp.transpose(cache_k, (0, 2, 3, 1)).reshape(n_phys, d, page)
    cache_v = jnp.transpose(cache_v, (0, 2, 3, 1)).reshape(n_phys, d, page)
    cache_logf = jnp.transpose(cache_logf, (0, 2, 1))

    batch = lambda i, p, j, pt: (i * hp + p, 0, 0)
    in_hbm = pl.BlockSpec(memory_space=pl.ANY)
    grid_spec = pltpu.PrefetchScalarGridSpec(
        num_scalar_prefetch=1,
        grid=(b, hp, n_tiles),
        in_specs=[pl.BlockSpec((1, 1, LANES, tq), lambda i, p, j, pt: (i, p, 0, j)),
                  pl.BlockSpec((1, 1, s, LANES), lambda i, p, j, pt: (i, p, 0, 0)),
                  pl.BlockSpec((1, s, LANES), lambda i, p, j, pt: (i, 0, 0)),
                  pl.BlockSpec((1, 1, LANES, s), lambda i, p, j, pt: (i, p, 0, 0)),
                  pl.BlockSpec((1, steps, d), batch), pl.BlockSpec((1, steps, d), batch),
                  pl.BlockSpec((1, steps, d), batch), pl.BlockSpec((1, steps, n_heads), batch),
                  in_hbm, in_hbm, in_hbm],
        out_specs=[pl.BlockSpec((1, tq, LANES), lambda i, p, j, pt: (i, j, p)),
                   pl.BlockSpec((1, steps, d), batch)],
        scratch_shapes=[pltpu.VMEM((2, tq, tq), F32), pltpu.VMEM((2, tq, tq), F32),
                        pltpu.VMEM((2, tq, tq), BF16), pltpu.VMEM((2, tq, tq), BF16),
                        pltpu.VMEM((rows, d + LANES), F32), pltpu.VMEM((rows, pages * page), F32),
                        pltpu.VMEM((rows, d), F32),
                        pltpu.VMEM((rows, 1), F32), pltpu.VMEM((rows, 1), F32),
                        pltpu.VMEM((n_heads, 1), F32), pltpu.VMEM((LANES, d), F32),
                        pltpu.VMEM((LANES, d), F32), pltpu.VMEM((LANES, LANES), F32),
                        pltpu.VMEM((2, pages, d, page), F32), pltpu.VMEM((2, pages, d, page), F32),
                        pltpu.VMEM((2, pages, n_heads, page), F32), pltpu.SemaphoreType.DMA((2, 3))],
    )
    return pl.pallas_call(
        functools.partial(_attn_kernel, tq=tq, n_heads=n_heads, pages=pages, steps=steps, d=d),
        grid_spec=grid_spec,
        out_shape=[jax.ShapeDtypeStruct((b, s, hp * LANES), BF16), jax.ShapeDtypeStruct((nb, steps, d), F32)],
        compiler_params=_params("arbitrary", "arbitrary", "arbitrary"),
        name="attention",
    )(page_table.reshape(-1), qtb, kb, fb16, vtb, q_s, k_new, v_new, lf_new, cache_k, cache_v, cache_logf)


def _outproj_kernel(o_ref, sz_ref, w_ref, postg_ref, gate_ref, x_ref, y_ref):
    g = (o_ref[0].astype(F32) * sz_ref[0].astype(F32)).astype(BF16)
    out = _dot(g, w_ref[...])
    y_ref[0] = x_ref[0] + gate_ref[0] * (_rms(out) * postg_ref[...])


def _outproj(o, sz, w, post_g, gate, x, *, tile, name):
    b, s, d = x.shape
    tok = pl.BlockSpec((1, tile, d), lambda i, j: (i, j, 0))
    per_row = gate.shape[1] == s
    mod = pl.BlockSpec((1, tile if per_row else 1, d),
                       (lambda i, j: (i, j, 0)) if per_row else (lambda i, j: (i, 0, 0)))
    return pl.pallas_call(
        _outproj_kernel,
        grid=(b, s // tile),
        in_specs=[tok, tok, _const_spec(w.shape), _const_spec((1, d)), mod, tok],
        out_specs=tok,
        out_shape=jax.ShapeDtypeStruct((b, s, d), F32),
        compiler_params=_params("arbitrary", "arbitrary"),
        name=name,
    )(o, sz, w, post_g, gate, x)


def kernel(x_prompt, x_sample, state_pool, cache_k, cache_v, cache_logf, page_table, c_prompt, c_sample,
           ada_w, ada_b, pre_g, post_g, a_in_w, a_grp_w, a_scale, a_out_w, kv_g, kv_w, f_b, b_in_w, b_out_w):
    bp, seq, d = x_prompt.shape
    bs, steps, _ = x_sample.shape
    n_heads = d // HEAD_DIM
    w_a = a_out_w.shape[1]
    rows_s = bs * steps

    pad = (-(bp + bs)) % 8
    c_all = jnp.concatenate([c_prompt, c_sample, jnp.zeros((pad, d), F32)], axis=0)
    mod = _adaln(c_all, ada_w, ada_b)

    def mods(layer):
        m = mod[layer]
        parts = [m[:, i * d:(i + 1) * d] for i in range(3)]
        prompt = [p[:bp].reshape(bp, 1, d) for p in parts]
        sample = [jnp.repeat(p[bp:bp + bs], steps, axis=0) for p in parts]
        return prompt, sample

    (shift0_p, scale0_p, gate0_p), (shift0_s, scale0_s, gate0_s) = mods(0)
    (shift1_p, scale1_p, gate1_p), (shift1_s, scale1_s, gate1_s) = mods(1)

    in_w = a_in_w[0].astype(BF16)
    grp_w = a_grp_w[0].astype(BF16)
    out_w = a_out_w[0].astype(BF16)
    kv_w_main = kv_w[:, :2 * d].astype(BF16)
    w_f = jnp.pad(kv_w[:, 2 * d:], ((0, 0), (0, LANES - n_heads))).astype(BF16)
    f_b_pad = jnp.pad(f_b, (0, LANES - n_heads)).reshape(1, LANES)
    bin_w = b_in_w[0].astype(BF16)
    bout_w = b_out_w[0].astype(BF16)
    pre0, pre1 = pre_g[0].reshape(1, d), pre_g[1].reshape(1, d)
    post0, post1 = post_g[0].reshape(1, d), post_g[1].reshape(1, d)
    kvg = kv_g.reshape(1, d)
    asc = a_scale[0].reshape(1, w_a)

    hist = jnp.transpose(state_pool[0], (1, 0, 2)).reshape(POOL_BUF * bs, w_a)
    x1_s, u_s = _pool_sample(x_sample.reshape(rows_s, d), shift0_s, scale0_s, gate0_s, pre0, post0,
                             in_w, hist, grp_w, asc, out_w, steps=steps)
    pool_sample = jnp.concatenate([state_pool[:, :, steps:], u_s.reshape(1, bs, steps, w_a)], axis=2)
    x1_s3 = x1_s.reshape(1, rows_s, d)
    k_s, v_s, sz_s, logf_s, q_s = _proj(
        x1_s3, shift1_s[None], scale1_s[None], kvg, pre1, kv_w_main, w_f, f_b_pad, bin_w,
        tile=rows_s, prompt=False)

    x1_p, tail_p = _pool_prompt(x_prompt, shift0_p, scale0_p, gate0_p, pre0, post0, in_w, grp_w, asc, out_w)
    pool_prompt = tail_p[None, :, HALO - POOL_BUF:, :]
    k_p, vt_p, sz_p, kb, vtb, qtb, fb16, lft_p = _proj(
        x1_p, shift1_p, scale1_p, kvg, pre1, kv_w_main, w_f, f_b_pad, bin_w, tile=512, prompt=True)
    logf_p = jnp.transpose(lft_p, (0, 2, 1))
    v_p = jnp.transpose(vt_p.reshape(bp, n_heads, HEAD_DIM, seq), (0, 3, 1, 2))

    o_p, o_s = _attention(qtb, kb, fb16, vtb, page_table, q_s.reshape(bs, steps, d), k_s.reshape(bs, steps, d),
                          v_s.reshape(bs, steps, d), logf_s.reshape(bs, steps, n_heads),
                          cache_k, cache_v, cache_logf, n_heads=n_heads)
    y_prompt = _outproj(o_p, sz_p, bout_w, post1, gate1_p, x1_p, tile=1024, name="outproj_prompt")
    y_sample = _outproj(o_s.reshape(1, rows_s, d), sz_s, bout_w, post1, gate1_s[None], x1_s3,
                        tile=rows_s, name="outproj_sample")

    return (y_prompt, y_sample.reshape(bs, steps, d), pool_prompt, pool_sample,
            k_p.reshape(bp, seq, n_heads, HEAD_DIM), v_p, logf_p,
            k_s.reshape(bs, steps, n_heads, HEAD_DIM), v_s.reshape(bs, steps, n_heads, HEAD_DIM),
            logf_s.reshape(bs, steps, n_heads))
```

```python
import functools

import jax
import jax.numpy as jnp
import numpy as np
from jax import lax
from jax.experimental import pallas as pl
from jax.experimental.pallas import tpu as pltpu

F32 = jnp.float32
BF16 = jnp.bfloat16

EPS = 1e-6
NEG = -1e30
LOG2E = 1.4426950408889634
POOL_WINDOWS = (2, 4, 8, 16)
POOL_BUF = max(POOL_WINDOWS) - 1
HEAD_DIM = 64
HALO = 16

V7X_VMEM_BYTES = 64 * 1024 * 1024
VMEM_LIMIT = V7X_VMEM_BYTES - 8 * 1024 * 1024
LANES = 128
BAND = 256
STRIP = 16


def _params(*sem):
    return pltpu.CompilerParams(dimension_semantics=sem, vmem_limit_bytes=VMEM_LIMIT)


def _const_spec(shape):
    nd = len(shape)
    return pl.BlockSpec(shape, lambda *_: (0,) * nd, pipeline_mode=pl.Buffered(1))


def _silu(x):
    return x * jax.nn.sigmoid(x)


def _rms(x):
    return x * lax.rsqrt(jnp.mean(x * x, axis=-1, keepdims=True) + EPS)


def _split3(x):
    a = x.astype(BF16)
    r = x - a.astype(F32)
    b = r.astype(BF16)
    c = (r - b.astype(F32)).astype(BF16)
    return a, b, c


def _log_sigmoid(x):
    return jnp.minimum(x, 0.0) - jnp.log1p(jnp.exp(-jnp.abs(x)))


def _div(x, n):
    return x >> (n.bit_length() - 1) if n & (n - 1) == 0 else x // n


def _mod(x, n):
    return x & (n - 1) if n & (n - 1) == 0 else x % n


def _dot(a, b):
    return jnp.dot(a, b, preferred_element_type=F32)


def _dot_nt(a, b):
    return lax.dot_general(a, b, (((1,), (1,)), ((), ())), preferred_element_type=F32)


def _dot_exact_rhs(sel, x):
    a, b, c = _split3(x)
    return _dot(sel, a) + _dot(sel, b) + _dot(sel, c)


def _dot_exact_lhs(x, sel):
    a, b, c = _split3(x)
    return _dot(a, sel) + _dot(b, sel) + _dot(c, sel)


def _adaln_kernel(c_ref, w_ref, b_ref, o_ref):
    a = _silu(c_ref[...]).astype(BF16)
    o_ref[0] = _dot(a, w_ref[0].astype(BF16)) + b_ref[0]


def _adaln(c_all, ada_w, ada_b):
    depth, d, d3 = ada_w.shape
    rows = c_all.shape[0]
    tn = d3 // 2
    return pl.pallas_call(
        _adaln_kernel,
        grid=(depth, d3 // tn),
        in_specs=[
            pl.BlockSpec((rows, d), lambda l, j: (0, 0)),
            pl.BlockSpec((1, d, tn), lambda l, j: (l, 0, j)),
            pl.BlockSpec((1, 1, tn), lambda l, j: (l, 0, j)),
        ],
        out_specs=pl.BlockSpec((1, rows, tn), lambda l, j: (l, 0, j)),
        out_shape=jax.ShapeDtypeStruct((depth, rows, d3), F32),
        compiler_params=_params("arbitrary", "arbitrary"),
        name="adaln",
    )(c_all, ada_w, ada_b.reshape(depth, 1, d3))


def _band_constants():
    t = np.arange(BAND)[:, None]
    s = np.arange(BAND)[None, :]
    eye = (t == s).astype(np.float32)

    def one(w, first):
        cnt = np.minimum(t + 1, w) if first else w
        return ((t - s >= 1) & (t - s < w)).astype(np.float32) - (cnt - 1) * eye

    band = np.stack([np.stack([one(w, first) for w in POOL_WINDOWS]) for first in (True, False)])
    th = np.arange(HALO)[:, None]
    jh = np.arange(HALO)[None, :]
    bandh = np.stack([(th - jh + HALO < w) for w in POOL_WINDOWS]).astype(np.float32)
    return jnp.asarray(band, BF16), jnp.asarray(bandh, BF16)


def _pool_prompt_kernel(x_ref, shift_ref, scale_ref, gate_ref, preg_ref, postg_ref, inw_ref,
                        band_ref, bandh_ref, grpw_ref, ascale_ref, outw_ref,
                        x1_ref, tail_ref, halo_ref, *, tile, w_a):
    t = pl.program_id(1)
    gw = w_a // len(POOL_WINDOWS)

    @pl.when(t == 0)
    def _():
        halo_ref[...] = jnp.zeros_like(halo_ref)

    x = x_ref[0]
    h = (_rms(x) * preg_ref[...]) * (1.0 + scale_ref[0]) + shift_ref[0]
    uz = _dot(h.astype(BF16), inw_ref[...])
    u = uz[:, :w_a]
    z = uz[:, w_a:]
    u_bf = u.astype(BF16)

    gated_blocks = []
    for blk in range(tile // BAND):
        r0 = blk * BAND
        halo = halo_ref[...] if blk == 0 else u_bf[r0 - HALO:r0]
        pos = t * tile + r0 + lax.broadcasted_iota(jnp.int32, (BAND, 1), 0)
        kind = jnp.where(t == 0, 0, 1) if blk == 0 else 1
        cols = []
        for g, w in enumerate(POOL_WINDOWS):
            c0 = g * gw
            ug = u_bf[r0:r0 + BAND, c0:c0 + gw]
            wsum = _dot(band_ref[kind, g], ug)
            top = wsum[:HALO] + _dot(bandh_ref[g], halo[:, c0:c0 + gw])
            wsum = jnp.concatenate([top, wsum[HALO:]], axis=0)
            cnt = jnp.minimum(pos + 1, w).astype(F32)
            pooled = wsum / cnt
            mixed = _dot(pooled.astype(BF16), grpw_ref[g]) * ascale_ref[:, c0:c0 + gw]
            cols.append((mixed * _silu(z[r0:r0 + BAND, c0:c0 + gw])).astype(BF16))
        gated_blocks.append(jnp.concatenate(cols, axis=1))
    gated = gated_blocks[0] if len(gated_blocks) == 1 else jnp.concatenate(gated_blocks, axis=0)

    y = _dot(gated, outw_ref[...])
    x1_ref[0] = x + gate_ref[0] * (_rms(y) * postg_ref[...])

    halo_ref[...] = u_bf[tile - HALO:]

    @pl.when(t == pl.num_programs(1) - 1)
    def _():
        tail_ref[0] = u[tile - HALO:]


def _pool_prompt(x, shift, scale, gate, pre_g, post_g, in_w, grp_w, a_scale, out_w, *, tile=512):
    b, s, d = x.shape
    w_a = out_w.shape[0]
    band, bandh = _band_constants()
    tok = pl.BlockSpec((1, tile, d), lambda i, j: (i, j, 0))
    mod = pl.BlockSpec((1, 1, d), lambda i, j: (i, 0, 0))
    return pl.pallas_call(
        functools.partial(_pool_prompt_kernel, tile=tile, w_a=w_a),
        grid=(b, s // tile),
        in_specs=[tok, mod, mod, mod, _const_spec((1, d)), _const_spec((1, d)),
                  _const_spec(in_w.shape), _const_spec(band.shape), _const_spec(bandh.shape),
                  _const_spec(grp_w.shape), _const_spec((1, w_a)), _const_spec(out_w.shape)],
        out_specs=[tok, pl.BlockSpec((1, HALO, w_a), lambda i, j: (i, 0, 0))],
        out_shape=[jax.ShapeDtypeStruct((b, s, d), F32),
                   jax.ShapeDtypeStruct((b, HALO, w_a), F32)],
        scratch_shapes=[pltpu.VMEM((HALO, w_a), BF16)],
        compiler_params=_params("arbitrary", "arbitrary"),
        name="pool_prompt",
    )(x, shift, scale, gate, pre_g, post_g, in_w, band, bandh, grp_w, a_scale, out_w)


def _pool_sample_kernel(x_ref, shift_ref, scale_ref, gate_ref, preg_ref, postg_ref, inw_ref,
                        hist_ref, grpw_ref, ascale_ref, outw_ref, x1_ref, u_ref, *, steps, w_a):
    gw = w_a // len(POOL_WINDOWS)
    rows = x_ref.shape[0]
    nh = hist_ref.shape[0]

    x = x_ref[...]
    h = (_rms(x) * preg_ref[...]) * (1.0 + scale_ref[...]) + shift_ref[...]
    uz = _dot(h.astype(BF16), inw_ref[...])
    u = uz[:, :w_a]
    z = uz[:, w_a:]
    u_ref[...] = u
    u_bf = u.astype(BF16)
    hist_bf = hist_ref[...].astype(BF16)

    n_batch = nh // POOL_BUF
    ro = lax.broadcasted_iota(jnp.int32, (rows, rows), 0)
    ri = lax.broadcasted_iota(jnp.int32, (rows, rows), 1)
    same = _div(ro, steps) == _div(ri, steps)
    lag_new = _mod(ro, steps) - _mod(ri, steps)
    roh = lax.broadcasted_iota(jnp.int32, (rows, nh), 0)
    ch = lax.broadcasted_iota(jnp.int32, (rows, nh), 1)
    in_batch = _mod(ch, n_batch) == _div(roh, steps)
    lag_hist = POOL_BUF + _mod(roh, steps) - _div(ch, n_batch)

    cols = []
    for g, w in enumerate(POOL_WINDOWS):
        c0 = g * gw
        sel_new = jnp.where(same & (lag_new >= 1) & (lag_new < w), 1.0, 0.0)
        sel_new = jnp.where(ro == ri, -(w - 1.0), sel_new).astype(BF16)
        sel_hist = jnp.where(in_batch & (lag_hist < w), 1.0, 0.0).astype(BF16)
        wsum = _dot(sel_new, u_bf[:, c0:c0 + gw]) + _dot(sel_hist, hist_bf[:, c0:c0 + gw])
        pooled = wsum / float(w)
        mixed = _dot(pooled.astype(BF16), grpw_ref[g]) * ascale_ref[:, c0:c0 + gw]
        cols.append((mixed * _silu(z[:, c0:c0 + gw])).astype(BF16))
    gated = jnp.concatenate(cols, axis=1)
    y = _dot(gated, outw_ref[...])
    x1_ref[...] = x + gate_ref[...] * (_rms(y) * postg_ref[...])


def _pool_sample(x, shift, scale, gate, pre_g, post_g, in_w, hist, grp_w, a_scale, out_w, *, steps):
    rows, d = x.shape
    w_a = out_w.shape[0]
    args = (x, shift, scale, gate, pre_g, post_g, in_w, hist, grp_w, a_scale, out_w)
    return pl.pallas_call(
        functools.partial(_pool_sample_kernel, steps=steps, w_a=w_a),
        grid=(1,),
        in_specs=[_const_spec(a.shape) for a in args],
        out_specs=[pl.BlockSpec((rows, d), lambda i: (0, 0)),
                   pl.BlockSpec((rows, w_a), lambda i: (0, 0))],
        out_shape=[jax.ShapeDtypeStruct((rows, d), F32),
                   jax.ShapeDtypeStruct((rows, w_a), F32)],
        compiler_params=_params("arbitrary"),
        name="pool_sample",
    )(*args)


def _proj_kernel(x_ref, shift_ref, scale_ref, kvg_ref, preg_ref, kvw_ref, wf_ref, fb_ref, binw_ref,
                 *rest, tile, d, prompt):
    if prompt:
        utri_ref, k_ref, vt_ref, sz_ref, kb_ref, vtb_ref, qtb_ref, fb16_ref, lft_ref, carry_ref = rest
    else:
        k_ref, v_ref, sz_ref, logf_ref, q_ref = rest
    n_heads = d // HEAD_DIM

    r = _rms(x_ref[0])
    xn = (r * kvg_ref[...]).astype(BF16)
    h = ((r * preg_ref[...]) * (1.0 + scale_ref[0]) + shift_ref[0]).astype(BF16)

    kv = _dot_nt(xn, kvw_ref[...])
    k = kv[:, :d]
    v = kv[:, d:]
    k_ref[0] = k
    logf = _log_sigmoid(_dot_nt(xn, wf_ref[...]) + fb_ref[...])

    qz = _dot(h, binw_ref[...])
    sz_ref[0] = _silu(qz[:, d:]).astype(BF16)

    if prompt:
        qt = (qz[:, :d] * (HEAD_DIM ** -0.5 * LOG2E)).T
        vt = v.T
        vt_ref[0] = vt
        for hp in range(d // LANES):
            sl = slice(hp * LANES, (hp + 1) * LANES)
            kb_ref[0, hp] = k[:, sl].astype(BF16)
            vtb_ref[0, hp] = vt[sl].astype(BF16)
            qtb_ref[0, hp] = qt[sl].astype(BF16)

        @pl.when(pl.program_id(1) == 0)
        def _():
            carry_ref[...] = jnp.zeros_like(carry_ref)

        lft = logf.T[:n_heads]
        lft_ref[0] = lft
        parts = _dot(jnp.concatenate(_split3(lft), axis=0), utri_ref[...])
        cum = parts[:n_heads] + parts[n_heads:2 * n_heads] + parts[2 * n_heads:] + carry_ref[...]
        carry_ref[...] = cum[:, tile - 1:tile]
        terms = jnp.concatenate([t.astype(F32) for t in _split3(cum * LOG2E)]
                                + [jnp.zeros((LANES - 3 * n_heads, tile), F32)], axis=0)
        fb16_ref[0] = terms.T.astype(BF16)
    else:
        v_ref[0] = v
        logf_ref[0] = logf[:, :n_heads]
        q_ref[0] = qz[:, :d] * (HEAD_DIM ** -0.5)


def _proj(x, shift, scale, kv_g, pre_g, kv_w_main, w_f, f_b, b_in_w, *, tile, prompt):
    b, s, d = x.shape
    n_heads = d // HEAD_DIM
    hp = d // LANES
    tok = pl.BlockSpec((1, tile, d), lambda i, j: (i, j, 0))
    mod_rows = shift.shape[1]
    mod_tile = tile if mod_rows == s else 1
    mod = pl.BlockSpec((1, mod_tile, d), (lambda i, j: (i, j, 0)) if mod_rows == s else (lambda i, j: (i, 0, 0)))
    in_specs = [tok, mod, mod, _const_spec((1, d)), _const_spec((1, d)), _const_spec(kv_w_main.shape),
                _const_spec(w_f.shape), _const_spec(f_b.shape), _const_spec(b_in_w.shape)]
    args = [x, shift, scale, kv_g, pre_g, kv_w_main, w_f, f_b, b_in_w]
    scratch = []
    if prompt:
        utri = jnp.asarray(np.triu(np.ones((tile, tile), np.float32)), BF16)
        in_specs.append(_const_spec(utri.shape))
        args.append(utri)
        pair = pl.BlockSpec((1, hp, tile, LANES), lambda i, j: (i, 0, j, 0))
        pair_t = pl.BlockSpec((1, hp, LANES, tile), lambda i, j: (i, 0, 0, j))
        out_specs = [tok, pl.BlockSpec((1, d, tile), lambda i, j: (i, 0, j)), tok, pair, pair_t, pair_t,
                     pl.BlockSpec((1, tile, LANES), lambda i, j: (i, j, 0)),
                     pl.BlockSpec((1, n_heads, tile), lambda i, j: (i, 0, j))]
        out_shape = [jax.ShapeDtypeStruct((b, s, d), F32), jax.ShapeDtypeStruct((b, d, s), F32),
                     jax.ShapeDtypeStruct((b, s, d), BF16), jax.ShapeDtypeStruct((b, hp, s, LANES), BF16),
                     jax.ShapeDtypeStruct((b, hp, LANES, s), BF16), jax.ShapeDtypeStruct((b, hp, LANES, s), BF16),
                     jax.ShapeDtypeStruct((b, s, LANES), BF16), jax.ShapeDtypeStruct((b, n_heads, s), F32)]
        scratch = [pltpu.VMEM((n_heads, 1), F32)]
    else:
        out_specs = [tok, tok, tok, pl.BlockSpec((1, tile, n_heads), lambda i, j: (i, j, 0)), tok]
        out_shape = [jax.ShapeDtypeStruct((b, s, d), F32), jax.ShapeDtypeStruct((b, s, d), F32),
                     jax.ShapeDtypeStruct((b, s, d), BF16), jax.ShapeDtypeStruct((b, s, n_heads), F32),
                     jax.ShapeDtypeStruct((b, s, d), F32)]
    return pl.pallas_call(
        functools.partial(_proj_kernel, tile=tile, d=d, prompt=prompt),
        grid=(b, s // tile),
        in_specs=in_specs,
        out_specs=out_specs,
        out_shape=out_shape,
        scratch_shapes=scratch,
        compiler_params=_params("arbitrary", "arbitrary"),
        name="proj_prompt" if prompt else "proj_sample",
    )(*args)


def _prompt_tile(qt_ref, k_ref, f_ref, vt_ref, o_ref, sa_ref, sb_ref, pa_ref, pb_ref, *, tq, n_heads,
                 early_work, late_work):
    tk = tq
    pair = pl.program_id(1)
    qi = pl.program_id(2)
    qt = qt_ref[0, 0]
    row = lax.broadcasted_iota(jnp.int32, (LANES, 1), 0)
    zero = jnp.zeros_like(qt)
    qat = []
    for hd in range(2):
        head = 2 * pair + hd
        mine = (row >= hd * HEAD_DIM) & (row < (hd + 1) * HEAD_DIM)
        pick = (row == head) | (row == n_heads + head) | (row == 2 * n_heads + head)
        minus = jnp.where(pick, -1.0, 0.0).astype(BF16)
        qat.append(jnp.concatenate([jnp.where(mine, qt, zero), jnp.broadcast_to(minus, (LANES, tq))], axis=0))

    def qk(tile, s_ref):
        k0 = pl.multiple_of(tile * tk, tk)
        ka = jnp.concatenate([k_ref[0, 0, pl.ds(k0, tk), :], f_ref[0, pl.ds(k0, tk), :]], axis=1)
        for hd in range(2):
            s_ref[hd] = _dot(ka, qat[hd])

    half = tq // 2

    def qk_diagonal(tile, s_ref):
        k0 = pl.multiple_of(tile * tk, tk)
        ka = jnp.concatenate([k_ref[0, 0, pl.ds(k0, tk), :], f_ref[0, pl.ds(k0, tk), :]], axis=1)
        for hd in range(2):
            s_ref[hd, :half, :] = _dot(ka[:half], qat[hd])
            s_ref[hd, half:, half:] = _dot(ka[half:], qat[hd][:, half:])

    ones = jnp.ones((8, tk), BF16)

    def pv(tile, p_ref, diagonal=False):
        k0 = pl.multiple_of(tile * tk, tk)
        vt = vt_ref[0, 0, :, pl.ds(k0, tk)]
        parts = []
        for hd in range(2):
            va = jnp.concatenate([vt[hd * HEAD_DIM:(hd + 1) * HEAD_DIM], ones], axis=0)
            if diagonal:
                early = _dot(va[:, :half], p_ref[hd, :half, :])
                late = _dot(va[:, half:], p_ref[hd, half:, half:])
                parts.append(jnp.concatenate([early[:, :half], early[:, half:] + late], axis=1))
            else:
                parts.append(_dot(va, p_ref[hd]))
        return parts

    def softmax(s_ref, p_ref, stats, diagonal=False):
        new_stats, alphas = [], []

        def scores(hd, r0):
            c0 = half if diagonal and r0 >= half else 0
            st = s_ref[hd, r0:r0 + STRIP, c0:]
            if diagonal:
                key = lax.broadcasted_iota(jnp.int32, st.shape, 0) + r0
                qry = lax.broadcasted_iota(jnp.int32, st.shape, 1) + c0
                st = jnp.where(key <= qry, st, NEG)
            return st, c0

        def fold(x, op):
            return op(x.reshape(STRIP // 8, 8, x.shape[1]), axis=0)

        for hd in range(2):
            m = stats[hd]
            top = jnp.full((8, tq), NEG, F32)
            for r0 in range(0, tk, STRIP):
                st, c0 = scores(hd, r0)
                upd = jnp.maximum(top[:, c0:], fold(st, jnp.max))
                top = upd if c0 == 0 else jnp.concatenate([top[:, :c0], upd], axis=1)
            m_new = jnp.maximum(m, jnp.max(top, axis=0, keepdims=True))
            alphas.append(jnp.exp2(m - m_new))
            for r0 in range(0, tk, STRIP):
                st, c0 = scores(hd, r0)
                p_ref[hd, r0:r0 + STRIP, c0:] = jnp.exp2((st - m_new[:, c0:]).astype(BF16))
            new_stats.append(m_new)
        return new_stats, alphas

    def rescale_add(alpha, acc, part):
        return [alpha[hd] * acc[hd] + part[hd] for hd in range(2)]

    def pv_before(a):
        return [jnp.where(a > 0, part, 0.0) for part in pv(jnp.maximum(a - 1, 0), pb_ref)]

    @pl.when((pl.program_id(0) == 0) & (pair == 0) & (qi == 0))
    def _():
        pb_ref[...] = jnp.zeros_like(pb_ref)

    early_work()
    qk(0, sa_ref)
    stats = [jnp.full((1, tq), NEG, F32) for _ in range(2)]
    alpha_b = [jnp.ones((1, tq), F32) for _ in range(2)]
    acc = [jnp.zeros((HEAD_DIM + 8, tq), F32) for _ in range(2)]

    def body(t, carry):
        stats, alpha_b, acc = carry
        a = 2 * t
        part = pv_before(a)
        qk(a + 1, sb_ref)
        stats, alpha_a = softmax(sa_ref, pa_ref, stats)
        acc = rescale_add(alpha_b, acc, part)
        part = pv(a, pa_ref)
        qk(a + 2, sa_ref)
        stats, alpha_b = softmax(sb_ref, pb_ref, stats)
        acc = rescale_add(alpha_a, acc, part)
        return stats, alpha_b, acc

    pairs = qi // 2
    stats, alpha_b, acc = lax.fori_loop(0, pairs, body, (stats, alpha_b, acc))
    a = 2 * pairs

    def finish(acc):
        o_ref[0] = jnp.concatenate(
            [acc[hd][:HEAD_DIM] / acc[hd][HEAD_DIM:HEAD_DIM + 1] for hd in range(2)], axis=0).T.astype(o_ref.dtype)

    @pl.when(a == qi)
    def _():
        part = pv_before(a)
        late_work()
        _, alpha_a = softmax(sa_ref, pa_ref, stats, diagonal=True)
        out = rescale_add(alpha_b, acc, part)
        finish(rescale_add(alpha_a, out, pv(a, pa_ref, diagonal=True)))

    @pl.when(a != qi)
    def _():
        part = pv_before(a)
        qk_diagonal(a + 1, sb_ref)
        late_work()
        mid, alpha_a = softmax(sa_ref, pa_ref, stats)
        out = rescale_add(alpha_b, acc, part)
        part = pv(a, pa_ref)
        _, alpha_d = softmax(sb_ref, pb_ref, mid, diagonal=True)
        out = rescale_add(alpha_a, out, part)
        finish(rescale_add(alpha_d, out, pv(a + 1, pb_ref, diagonal=True)))


def _forget_prefix(blocks, carry):
    n_heads, page = blocks[0].shape
    stacked = jnp.concatenate(blocks, axis=0)
    r = lax.broadcasted_iota(jnp.int32, (page, page), 0)
    c = lax.broadcasted_iota(jnp.int32, (page, page), 1)
    within = _dot_exact_lhs(stacked, jnp.where(r <= c, 1.0, 0.0).astype(BF16))
    sums = []
    for b in range(len(blocks)):
        blk = within[b * n_heads:(b + 1) * n_heads] + carry
        sums.append(blk)
        carry = blk[:, page - 1:page]
    return sums, carry


def _sample_chunk(chunk, last, very_first, q_ref, knew_ref, vnew_ref, lfnew_ref, k_pages, v_pages, f_pages,
                  o_ref, qbd_ref, st_ref, acc_ref, m_ref, l_ref, fc_ref, knew_buf, vnew_buf, lfnew_buf,
                  *, steps, d):
    n_heads = d // HEAD_DIM
    page = k_pages[0].shape[-1]
    rows = n_heads * steps
    assert page == LANES and rows % 8 == 0 and 3 * n_heads <= LANES

    row_id = lax.broadcasted_iota(jnp.int32, (rows, 1), 0)
    row_head = _div(row_id, steps)
    row_step = _mod(row_id, steps)
    lane_head = _div(lax.broadcasted_iota(jnp.int32, (1, d), 1), HEAD_DIM)

    @pl.when(very_first)
    def _():
        knew_buf[...] = jnp.zeros_like(knew_buf)
        vnew_buf[...] = jnp.zeros_like(vnew_buf)
        lfnew_buf[...] = jnp.zeros_like(lfnew_buf)

    @pl.when(chunk == 0)
    def _():
        q = q_ref[0]
        qbd = jnp.zeros((rows, d), F32)
        for i in range(steps):
            qbd = jnp.where((lane_head == row_head) & (row_step == i), q[i:i + 1, :], qbd)
        lane = lax.broadcasted_iota(jnp.int32, (1, LANES), 1)
        pick = (lane == row_head) | (lane == n_heads + row_head) | (lane == 2 * n_heads + row_head)
        qbd_ref[...] = jnp.concatenate([qbd, jnp.where(pick, -1.0, 0.0)], axis=1)
        m_ref[...] = jnp.full_like(m_ref, NEG)
        l_ref[...] = jnp.zeros_like(l_ref)
        acc_ref[...] = jnp.zeros_like(acc_ref)
        fc_ref[...] = jnp.zeros_like(fc_ref)

    def online_update(st, value_dot):
        m = m_ref[...]
        m_new = jnp.maximum(m, jnp.max(st, axis=1, keepdims=True))
        alpha = jnp.exp(m - m_new)
        p = jnp.exp(st - m_new)
        l_ref[...] = alpha * l_ref[...] + jnp.sum(p, axis=1, keepdims=True)
        m_ref[...] = m_new
        acc_ref[...] = alpha * acc_ref[...] + value_dot(p)

    def scores():
        sums, carry = _forget_prefix([r[...] for r in f_pages], fc_ref[...])
        fc_ref[...] = carry
        hi, mid, lo = _split3(jnp.concatenate(sums, axis=1))
        keys = hi.shape[1]
        terms = jnp.concatenate([hi.astype(F32), mid.astype(F32), lo.astype(F32),
                                 jnp.zeros((LANES - 3 * n_heads, keys), F32)], axis=0)
        kt = jnp.concatenate([jnp.concatenate([r[...] for r in k_pages], axis=1), terms], axis=0)
        st_ref[...] = _dot(qbd_ref[...], kt)

    def update():
        vt = jnp.concatenate([r[...] for r in v_pages], axis=1)
        online_update(st_ref[...], lambda p: _dot_nt(p, vt))

    def finish():
        @pl.when(chunk == last)
        def _():
            knew_buf[0:steps, :] = knew_ref[0]
            vnew_buf[0:steps, :] = vnew_ref[0]
            lfnew_buf[0:steps, 0:n_heads] = lfnew_ref[0]
            new_sums, _ = _forget_prefix([lfnew_buf[...].T[:n_heads]], fc_ref[...])
            expand = jnp.where(lax.broadcasted_iota(jnp.int32, (rows, n_heads), 1) == row_head,
                               1.0, 0.0).astype(BF16)
            st = _dot_nt(qbd_ref[:, :d], knew_buf[...]) - _dot_exact_rhs(expand, new_sums[0])
            key = lax.broadcasted_iota(jnp.int32, (1, LANES), 1)
            online_update(jnp.where(key <= row_step, st, NEG), lambda p: _dot(p, vnew_buf[...]))

            res = jnp.where(lane_head == row_head, acc_ref[...] / l_ref[...], 0.0)
            gather = jnp.where(lax.broadcasted_iota(jnp.int32, (8, rows), 0) ==
                               _mod(lax.broadcasted_iota(jnp.int32, (8, rows), 1), steps), 1.0, 0.0).astype(BF16)
            o_ref[0] = _dot_exact_rhs(gather, res)[:steps]

    return scores, update, finish


def _attn_kernel(pt_ref, qt_ref, k_ref, f_ref, vt_ref, qs_ref, knew_ref, vnew_ref, lfnew_ref, *rest,
                 tq, n_heads, pages, steps, d):
    (ck_hbm, cv_hbm, cf_hbm, o_ref, os_ref, sa_ref, sb_ref, pa_ref, pb_ref,
     qbd_ref, st_ref, acc_ref, m_ref, l_ref, fc_ref, knew_buf, vnew_buf, lfnew_buf,
     kbuf, vbuf, fbuf, sems) = rest
    chunk = pl.program_id(2)
    n_chunks = pl.num_programs(2)
    step = (pl.program_id(0) * pl.num_programs(1) + pl.program_id(1)) * n_chunks + chunk
    n_steps = pl.num_programs(0) * pl.num_programs(1) * n_chunks
    slot = lax.rem(step, 2)

    def page_copies(of_step, into):
        copies = []
        for r in range(pages):
            pg = pt_ref[of_step * pages + r]
            copies += [pltpu.make_async_copy(ck_hbm.at[pg], kbuf.at[into, r], sems.at[into, 0]),
                       pltpu.make_async_copy(cv_hbm.at[pg], vbuf.at[into, r], sems.at[into, 1]),
                       pltpu.make_async_copy(cf_hbm.at[pg], fbuf.at[into, r], sems.at[into, 2])]
        return copies

    @pl.when(step == 0)
    def _():
        for c in page_copies(0, 0):
            c.start()

    def page_sync():
        for c in page_copies(step, slot):
            c.wait()
        for c in page_copies(jnp.where(step + 1 < n_steps, step + 1, 0), 1 - slot):
            c.start()

    k_pages = [kbuf.at[slot, r] for r in range(pages)]
    v_pages = [vbuf.at[slot, r] for r in range(pages)]
    f_pages = [fbuf.at[slot, r] for r in range(pages)]
    very_first = step == 0
    sample_scores, sample_update, sample_finish = _sample_chunk(
        chunk, n_chunks - 1, very_first, qs_ref, knew_ref, vnew_ref, lfnew_ref,
        k_pages, v_pages, f_pages, os_ref, qbd_ref, st_ref, acc_ref, m_ref, l_ref, fc_ref,
        knew_buf, vnew_buf, lfnew_buf, steps=steps, d=d)

    def early_work():
        page_sync()
        sample_scores()

    _prompt_tile(qt_ref, k_ref, f_ref, vt_ref, o_ref, sa_ref, sb_ref, pa_ref, pb_ref, tq=tq, n_heads=n_heads,
                 early_work=early_work, late_work=sample_update)
    sample_finish()

    @pl.when(step == n_steps - 1)
    def _():
        for c in page_copies(0, 1 - slot):
            c.wait()


def _attention(qtb, kb, fb16, vtb, page_table, q_s, k_new, v_new, lf_new, cache_k, cache_v, cache_logf,
               *, n_heads, tq=512):
    b, hp, s, _ = kb.shape
    nb, steps, d = q_s.shape
    n_pages = page_table.shape[1]
    n_phys, page, _ = cache_logf.shape
    rows = n_heads * steps
    n_tiles = s // tq
    assert nb == b * hp and n_pages % n_tiles == 0
    pages = n_pages // n_tiles
    cache_k = jnp.transpose(cache_k, (0, 2, 3, 1)).reshape(n_phys, d, page)
    cache_v = jnp.transpose(cache_v, (0, 2, 3, 1)).reshape(n_phys, d, page)
    cache_logf = jnp.transpose(cache_logf, (0, 2, 1))

    batch = lambda i, p, j, pt: (i * hp + p, 0, 0)
    in_hbm = pl.BlockSpec(memory_space=pl.ANY)
    grid_spec = pltpu.PrefetchScalarGridSpec(
        num_scalar_prefetch=1,
        grid=(b, hp, n_tiles),
        in_specs=[pl.BlockSpec((1, 1, LANES, tq), lambda i, p, j, pt: (i, p, 0, j)),
                  pl.BlockSpec((1, 1, s, LANES), lambda i, p, j, pt: (i, p, 0, 0)),
                  pl.BlockSpec((1, s, LANES), lambda i, p, j, pt: (i, 0, 0)),
                  pl.BlockSpec((1, 1, LANES, s), lambda i, p, j, pt: (i, p, 0, 0)),
                  pl.BlockSpec((1, steps, d), batch), pl.BlockSpec((1, steps, d), batch),
                  pl.BlockSpec((1, steps, d), batch), pl.BlockSpec((1, steps, n_heads), batch),
                  in_hbm, in_hbm, in_hbm],
        out_specs=[pl.BlockSpec((1, tq, LANES), lambda i, p, j, pt: (i, j, p)),
                   pl.BlockSpec((1, steps, d), batch)],
        scratch_shapes=[pltpu.VMEM((2, tq, tq), F32), pltpu.VMEM((2, tq, tq), F32),
                        pltpu.VMEM((2, tq, tq), BF16), pltpu.VMEM((2, tq, tq), BF16),
                        pltpu.VMEM((rows, d + LANES), F32), pltpu.VMEM((rows, pages * page), F32),
                        pltpu.VMEM((rows, d), F32),
                        pltpu.VMEM((rows, 1), F32), pltpu.VMEM((rows, 1), F32),
                        pltpu.VMEM((n_heads, 1), F32), pltpu.VMEM((LANES, d), F32),
                        pltpu.VMEM((LANES, d), F32), pltpu.VMEM((LANES, LANES), F32),
                        pltpu.VMEM((2, pages, d, page), F32), pltpu.VMEM((2, pages, d, page), F32),
                        pltpu.VMEM((2, pages, n_heads, page), F32), pltpu.SemaphoreType.DMA((2, 3))],
    )
    return pl.pallas_call(
        functools.partial(_attn_kernel, tq=tq, n_heads=n_heads, pages=pages, steps=steps, d=d),
        grid_spec=grid_spec,
        out_shape=[jax.ShapeDtypeStruct((b, s, hp * LANES), BF16), jax.ShapeDtypeStruct((nb, steps, d), F32)],
        compiler_params=_params("arbitrary", "arbitrary", "arbitrary"),
        name="attention",
    )(page_table.reshape(-1), qtb, kb, fb16, vtb, q_s, k_new, v_new, lf_new, cache_k, cache_v, cache_logf)


def _outproj_kernel(o_ref, sz_ref, w_ref, postg_ref, gate_ref, x_ref, y_ref):
    g = (o_ref[0].astype(F32) * sz_ref[0].astype(F32)).astype(BF16)
    out = _dot(g, w_ref[...])
    y_ref[0] = x_ref[0] + gate_ref[0] * (_rms(out) * postg_ref[...])


def _outproj(o, sz, w, post_g, gate, x, *, tile, name):
    b, s, d = x.shape
    tok = pl.BlockSpec((1, tile, d), lambda i, j: (i, j, 0))
    per_row = gate.shape[1] == s
    mod = pl.BlockSpec((1, tile if per_row else 1, d),
                       (lambda i, j: (i, j, 0)) if per_row else (lambda i, j: (i, 0, 0)))
    return pl.pallas_call(
        _outproj_kernel,
        grid=(b, s // tile),
        in_specs=[tok, tok, _const_spec(w.shape), _const_spec((1, d)), mod, tok],
        out_specs=tok,
        out_shape=jax.ShapeDtypeStruct((b, s, d), F32),
        compiler_params=_params("arbitrary", "arbitrary"),
        name=name,
    )(o, sz, w, post_g, gate, x)


def kernel(x_prompt, x_sample, state_pool, cache_k, cache_v, cache_logf, page_table, c_prompt, c_sample,
           ada_w, ada_b, pre_g, post_g, a_in_w, a_grp_w, a_scale, a_out_w, kv_g, kv_w, f_b, b_in_w, b_out_w):
    bp, seq, d = x_prompt.shape
    bs, steps, _ = x_sample.shape
    n_heads = d // HEAD_DIM
    w_a = a_out_w.shape[1]
    rows_s = bs * steps

    pad = (-(bp + bs)) % 8
    c_all = jnp.concatenate([c_prompt, c_sample, jnp.zeros((pad, d), F32)], axis=0)
    mod = _adaln(c_all, ada_w, ada_b)

    def mods(layer):
        m = mod[layer]
        parts = [m[:, i * d:(i + 1) * d] for i in range(3)]
        prompt = [p[:bp].reshape(bp, 1, d) for p in parts]
        sample = [jnp.repeat(p[bp:bp + bs], steps, axis=0) for p in parts]
        return prompt, sample

    (shift0_p, scale0_p, gate0_p), (shift0_s, scale0_s, gate0_s) = mods(0)
    (shift1_p, scale1_p, gate1_p), (shift1_s, scale1_s, gate1_s) = mods(1)

    in_w = a_in_w[0].astype(BF16)
    grp_w = a_grp_w[0].astype(BF16)
    out_w = a_out_w[0].astype(BF16)
    kv_wt = kv_w.T
    kv_w_main = kv_wt[:2 * d].astype(BF16)
    w_f = jnp.pad(kv_wt[2 * d:], ((0, LANES - n_heads), (0, 0))).astype(BF16)
    f_b_pad = jnp.pad(f_b, (0, LANES - n_heads)).reshape(1, LANES)
    bin_w = b_in_w[0].astype(BF16)
    bout_w = b_out_w[0].astype(BF16)
    pre0, pre1 = pre_g[0].reshape(1, d), pre_g[1].reshape(1, d)
    post0, post1 = post_g[0].reshape(1, d), post_g[1].reshape(1, d)
    kvg = kv_g.reshape(1, d)
    asc = a_scale[0].reshape(1, w_a)

    hist = jnp.transpose(state_pool[0], (1, 0, 2)).reshape(POOL_BUF * bs, w_a)
    x1_s, u_s = _pool_sample(x_sample.reshape(rows_s, d), shift0_s, scale0_s, gate0_s, pre0, post0,
                             in_w, hist, grp_w, asc, out_w, steps=steps)
    pool_sample = jnp.concatenate([state_pool[:, :, steps:], u_s.reshape(1, bs, steps, w_a)], axis=2)
    x1_s3 = x1_s.reshape(1, rows_s, d)
    k_s, v_s, sz_s, logf_s, q_s = _proj(
        x1_s3, shift1_s[None], scale1_s[None], kvg, pre1, kv_w_main, w_f, f_b_pad, bin_w,
        tile=rows_s, prompt=False)

    x1_p, tail_p = _pool_prompt(x_prompt, shift0_p, scale0_p, gate0_p, pre0, post0, in_w, grp_w, asc, out_w)
    pool_prompt = tail_p[None, :, HALO - POOL_BUF:, :]
    k_p, vt_p, sz_p, kb, vtb, qtb, fb16, lft_p = _proj(
        x1_p, shift1_p, scale1_p, kvg, pre1, kv_w_main, w_f, f_b_pad, bin_w, tile=512, prompt=True)
    logf_p = jnp.transpose(lft_p, (0, 2, 1))
    v_p = jnp.transpose(vt_p.reshape(bp, n_heads, HEAD_DIM, seq), (0, 3, 1, 2))

    o_p, o_s = _attention(qtb, kb, fb16, vtb, page_table, q_s.reshape(bs, steps, d), k_s.reshape(bs, steps, d),
                          v_s.reshape(bs, steps, d), logf_s.reshape(bs, steps, n_heads),
                          cache_k, cache_v, cache_logf, n_heads=n_heads)
    y_prompt = _outproj(o_p, sz_p, bout_w, post1, gate1_p, x1_p, tile=1024, name="outproj_prompt")
    y_sample = _outproj(o_s.reshape(1, rows_s, d), sz_s, bout_w, post1, gate1_s[None], x1_s3,
                        tile=rows_s, name="outproj_sample")

    return (y_prompt, y_sample.reshape(bs, steps, d), pool_prompt, pool_sample,
            k_p.reshape(bp, seq, n_heads, HEAD_DIM), v_p, logf_p,
            k_s.reshape(bs, steps, n_heads, HEAD_DIM), v_s.reshape(bs, steps, n_heads, HEAD_DIM),
            logf_s.reshape(bs, steps, n_heads))
```

```python
import functools

import jax
import jax.numpy as jnp
import numpy as np
from jax import lax
from jax.experimental import pallas as pl
from jax.experimental.pallas import tpu as pltpu

F32 = jnp.float32
BF16 = jnp.bfloat16

EPS = 1e-6
NEG = -1e30
LOG2E = 1.4426950408889634
POOL_WINDOWS = (2, 4, 8, 16)
POOL_BUF = max(POOL_WINDOWS) - 1
HEAD_DIM = 64
HALO = 16

V7X_VMEM_BYTES = 64 * 1024 * 1024
VMEM_LIMIT = V7X_VMEM_BYTES - 8 * 1024 * 1024
LANES = 128
BAND = 256
STRIP = 16


def _params(*sem):
    return pltpu.CompilerParams(dimension_semantics=sem, vmem_limit_bytes=VMEM_LIMIT)


def _const_spec(shape):
    nd = len(shape)
    return pl.BlockSpec(shape, lambda *_: (0,) * nd, pipeline_mode=pl.Buffered(1))


def _silu(x):
    return x * jax.nn.sigmoid(x)


def _rms(x):
    return x * lax.rsqrt(jnp.mean(x * x, axis=-1, keepdims=True) + EPS)


def _split3(x):
    a = x.astype(BF16)
    r = x - a.astype(F32)
    b = r.astype(BF16)
    c = (r - b.astype(F32)).astype(BF16)
    return a, b, c


def _log_sigmoid(x):
    return jnp.minimum(x, 0.0) - jnp.log1p(jnp.exp(-jnp.abs(x)))


def _div(x, n):
    return x >> (n.bit_length() - 1) if n & (n - 1) == 0 else x // n


def _mod(x, n):
    return x & (n - 1) if n & (n - 1) == 0 else x % n


def _dot(a, b):
    return jnp.dot(a, b, preferred_element_type=F32)


def _dot_nt(a, b):
    return lax.dot_general(a, b, (((1,), (1,)), ((), ())), preferred_element_type=F32)


def _dot_exact_rhs(sel, x):
    a, b, c = _split3(x)
    return _dot(sel, a) + _dot(sel, b) + _dot(sel, c)


def _dot_exact_lhs(x, sel):
    a, b, c = _split3(x)
    return _dot(a, sel) + _dot(b, sel) + _dot(c, sel)


def _adaln_kernel(c_ref, w_ref, b_ref, o_ref):
    a = _silu(c_ref[...]).astype(BF16)
    o_ref[0] = _dot(a, w_ref[0].astype(BF16)) + b_ref[0]


def _adaln(c_all, ada_w, ada_b):
    depth, d, d3 = ada_w.shape
    rows = c_all.shape[0]
    tn = d3 // 2
    return pl.pallas_call(
        _adaln_kernel,
        grid=(depth, d3 // tn),
        in_specs=[
            pl.BlockSpec((rows, d), lambda l, j: (0, 0)),
            pl.BlockSpec((1, d, tn), lambda l, j: (l, 0, j)),
            pl.BlockSpec((1, 1, tn), lambda l, j: (l, 0, j)),
        ],
        out_specs=pl.BlockSpec((1, rows, tn), lambda l, j: (l, 0, j)),
        out_shape=jax.ShapeDtypeStruct((depth, rows, d3), F32),
        compiler_params=_params("arbitrary", "arbitrary"),
        name="adaln",
    )(c_all, ada_w, ada_b.reshape(depth, 1, d3))


def _band_constants():
    t = np.arange(BAND)[:, None]
    s = np.arange(BAND)[None, :]
    eye = (t == s).astype(np.float32)

    def one(w, first):
        cnt = np.minimum(t + 1, w) if first else w
        return ((t - s >= 1) & (t - s < w)).astype(np.float32) - (cnt - 1) * eye

    band = np.stack([np.stack([one(w, first) for w in POOL_WINDOWS]) for first in (True, False)])
    th = np.arange(HALO)[:, None]
    jh = np.arange(HALO)[None, :]
    bandh = np.stack([(th - jh + HALO < w) for w in POOL_WINDOWS]).astype(np.float32)
    return jnp.asarray(band, BF16), jnp.asarray(bandh, BF16)


def _pool_prompt_kernel(x_ref, shift_ref, scale_ref, gate_ref, preg_ref, postg_ref, inw_ref,
                        band_ref, bandh_ref, grpw_ref, ascale_ref, outw_ref,
                        x1_ref, tail_ref, halo_ref, *, tile, w_a):
    t = pl.program_id(1)
    gw = w_a // len(POOL_WINDOWS)

    @pl.when(t == 0)
    def _():
        halo_ref[...] = jnp.zeros_like(halo_ref)

    x = x_ref[0]
    h = (_rms(x) * preg_ref[...]) * (1.0 + scale_ref[0]) + shift_ref[0]
    uz = _dot(h.astype(BF16), inw_ref[...])
    u = uz[:, :w_a]
    z = uz[:, w_a:]
    u_bf = u.astype(BF16)

    gated_blocks = []
    for blk in range(tile // BAND):
        r0 = blk * BAND
        halo = halo_ref[...] if blk == 0 else u_bf[r0 - HALO:r0]
        pos = t * tile + r0 + lax.broadcasted_iota(jnp.int32, (BAND, 1), 0)
        kind = jnp.where(t == 0, 0, 1) if blk == 0 else 1
        cols = []
        for g, w in enumerate(POOL_WINDOWS):
            c0 = g * gw
            ug = u_bf[r0:r0 + BAND, c0:c0 + gw]
            wsum = _dot(band_ref[kind, g], ug)
            top = wsum[:HALO] + _dot(bandh_ref[g], halo[:, c0:c0 + gw])
            wsum = jnp.concatenate([top, wsum[HALO:]], axis=0)
            cnt = jnp.minimum(pos + 1, w).astype(F32)
            pooled = wsum / cnt
            mixed = _dot(pooled.astype(BF16), grpw_ref[g]) * ascale_ref[:, c0:c0 + gw]
            cols.append((mixed * _silu(z[r0:r0 + BAND, c0:c0 + gw])).astype(BF16))
        gated_blocks.append(jnp.concatenate(cols, axis=1))
    gated = gated_blocks[0] if len(gated_blocks) == 1 else jnp.concatenate(gated_blocks, axis=0)

    y = _dot(gated, outw_ref[...])
    x1_ref[0] = x + gate_ref[0] * (_rms(y) * postg_ref[...])

    halo_ref[...] = u_bf[tile - HALO:]

    @pl.when(t == pl.num_programs(1) - 1)
    def _():
        tail_ref[0] = u[tile - HALO:]


def _pool_prompt(x, shift, scale, gate, pre_g, post_g, in_w, grp_w, a_scale, out_w, *, tile=512):
    b, s, d = x.shape
    w_a = out_w.shape[0]
    band, bandh = _band_constants()
    tok = pl.BlockSpec((1, tile, d), lambda i, j: (i, j, 0))
    mod = pl.BlockSpec((1, 1, d), lambda i, j: (i, 0, 0))
    return pl.pallas_call(
        functools.partial(_pool_prompt_kernel, tile=tile, w_a=w_a),
        grid=(b, s // tile),
        in_specs=[tok, mod, mod, mod, _const_spec((1, d)), _const_spec((1, d)),
                  _const_spec(in_w.shape), _const_spec(band.shape), _const_spec(bandh.shape),
                  _const_spec(grp_w.shape), _const_spec((1, w_a)), _const_spec(out_w.shape)],
        out_specs=[tok, pl.BlockSpec((1, HALO, w_a), lambda i, j: (i, 0, 0))],
        out_shape=[jax.ShapeDtypeStruct((b, s, d), F32),
                   jax.ShapeDtypeStruct((b, HALO, w_a), F32)],
        scratch_shapes=[pltpu.VMEM((HALO, w_a), BF16)],
        compiler_params=_params("arbitrary", "arbitrary"),
        name="pool_prompt",
    )(x, shift, scale, gate, pre_g, post_g, in_w, band, bandh, grp_w, a_scale, out_w)


def _pool_sample_kernel(x_ref, shift_ref, scale_ref, gate_ref, preg_ref, postg_ref, inw_ref,
                        hist_ref, grpw_ref, ascale_ref, outw_ref, x1_ref, u_ref, *, steps, w_a):
    gw = w_a // len(POOL_WINDOWS)
    rows = x_ref.shape[0]
    nh = hist_ref.shape[0]

    x = x_ref[...]
    h = (_rms(x) * preg_ref[...]) * (1.0 + scale_ref[...]) + shift_ref[...]
    uz = _dot(h.astype(BF16), inw_ref[...])
    u = uz[:, :w_a]
    z = uz[:, w_a:]
    u_ref[...] = u
    u_bf = u.astype(BF16)
    hist_bf = hist_ref[...].astype(BF16)

    n_batch = nh // POOL_BUF
    ro = lax.broadcasted_iota(jnp.int32, (rows, rows), 0)
    ri = lax.broadcasted_iota(jnp.int32, (rows, rows), 1)
    same = _div(ro, steps) == _div(ri, steps)
    lag_new = _mod(ro, steps) - _mod(ri, steps)
    roh = lax.broadcasted_iota(jnp.int32, (rows, nh), 0)
    ch = lax.broadcasted_iota(jnp.int32, (rows, nh), 1)
    in_batch = _mod(ch, n_batch) == _div(roh, steps)
    lag_hist = POOL_BUF + _mod(roh, steps) - _div(ch, n_batch)

    cols = []
    for g, w in enumerate(POOL_WINDOWS):
        c0 = g * gw
        sel_new = jnp.where(same & (lag_new >= 1) & (lag_new < w), 1.0, 0.0)
        sel_new = jnp.where(ro == ri, -(w - 1.0), sel_new).astype(BF16)
        sel_hist = jnp.where(in_batch & (lag_hist < w), 1.0, 0.0).astype(BF16)
        wsum = _dot(sel_new, u_bf[:, c0:c0 + gw]) + _dot(sel_hist, hist_bf[:, c0:c0 + gw])
        pooled = wsum / float(w)
        mixed = _dot(pooled.astype(BF16), grpw_ref[g]) * ascale_ref[:, c0:c0 + gw]
        cols.append((mixed * _silu(z[:, c0:c0 + gw])).astype(BF16))
    gated = jnp.concatenate(cols, axis=1)
    y = _dot(gated, outw_ref[...])
    x1_ref[...] = x + gate_ref[...] * (_rms(y) * postg_ref[...])


def _sample_front_kernel(*refs, steps, w_a, d, n_pool_in, n_proj_in):
    pool_in = refs[:n_pool_in]
    proj_in = refs[n_pool_in:n_pool_in + n_proj_in]
    x1_ref, u_ref, k_ref, v_ref, sz_ref, logf_ref, q_ref = refs[n_pool_in + n_proj_in:]
    _pool_sample_kernel(*pool_in, x1_ref.at[0], u_ref, steps=steps, w_a=w_a)
    _proj_kernel(x1_ref, *proj_in, k_ref, v_ref, sz_ref, logf_ref, q_ref,
                 tile=x1_ref.shape[1], d=d, prompt=False)


def _sample_front(pool_args, proj_args, *, steps):
    x = pool_args[0]
    rows, d = x.shape
    w_a = pool_args[-1].shape[0]
    n_heads = d // HEAD_DIM
    args = tuple(pool_args) + tuple(proj_args)
    whole = lambda shape: pl.BlockSpec(shape, lambda i: (0,) * len(shape))
    tok = (1, rows, d)
    out_shapes = [(tok, F32), ((rows, w_a), F32), (tok, F32), (tok, F32), (tok, BF16),
                  ((1, rows, n_heads), F32), (tok, F32)]
    return pl.pallas_call(
        functools.partial(_sample_front_kernel, steps=steps, w_a=w_a, d=d,
                          n_pool_in=len(pool_args), n_proj_in=len(proj_args)),
        grid=(1,),
        in_specs=[_const_spec(a.shape) for a in args],
        out_specs=[whole(s) for s, _ in out_shapes],
        out_shape=[jax.ShapeDtypeStruct(s, t) for s, t in out_shapes],
        compiler_params=_params("arbitrary"),
        name="sample_front",
    )(*args)


def _proj_kernel(x_ref, shift_ref, scale_ref, kvg_ref, preg_ref, kvw_ref, wf_ref, fb_ref, binw_ref,
                 *rest, tile, d, prompt):
    if prompt:
        utri_ref, k_ref, vt_ref, sz_ref, kb_ref, vtb_ref, qtb_ref, fb16_ref, lft_ref, carry_ref = rest
    else:
        k_ref, v_ref, sz_ref, logf_ref, q_ref = rest
    n_heads = d // HEAD_DIM

    r = _rms(x_ref[0])
    xn = (r * kvg_ref[...]).astype(BF16)
    h = ((r * preg_ref[...]) * (1.0 + scale_ref[0]) + shift_ref[0]).astype(BF16)

    kv = _dot_nt(xn, kvw_ref[...])
    k = kv[:, :d]
    v = kv[:, d:]
    k_ref[0] = k
    logf = _log_sigmoid(_dot_nt(xn, wf_ref[...]) + fb_ref[...])

    qz = _dot(h, binw_ref[...])
    sz_ref[0] = _silu(qz[:, d:]).astype(BF16)

    if prompt:
        qt = (qz[:, :d] * (HEAD_DIM ** -0.5 * LOG2E)).T
        vt = v.T
        vt_ref[0] = vt
        for hp in range(d // LANES):
            sl = slice(hp * LANES, (hp + 1) * LANES)
            kb_ref[0, hp] = k[:, sl].astype(BF16)
            vtb_ref[0, hp] = vt[sl].astype(BF16)
            qtb_ref[0, hp] = qt[sl].astype(BF16)

        @pl.when(pl.program_id(1) == 0)
        def _():
            carry_ref[...] = jnp.zeros_like(carry_ref)

        lft = logf.T[:n_heads]
        lft_ref[0] = lft
        parts = _dot(jnp.concatenate(_split3(lft), axis=0), utri_ref[...])
        cum = parts[:n_heads] + parts[n_heads:2 * n_heads] + parts[2 * n_heads:] + carry_ref[...]
        carry_ref[...] = cum[:, tile - 1:tile]
        terms = jnp.concatenate([t.astype(F32) for t in _split3(cum * LOG2E)]
                                + [jnp.zeros((LANES - 3 * n_heads, tile), F32)], axis=0)
        fb16_ref[0] = terms.T.astype(BF16)
    else:
        v_ref[0] = v
        logf_ref[0] = logf[:, :n_heads]
        q_ref[0] = qz[:, :d] * (HEAD_DIM ** -0.5)


def _proj(x, shift, scale, kv_g, pre_g, kv_w_main, w_f, f_b, b_in_w, *, tile, prompt):
    b, s, d = x.shape
    n_heads = d // HEAD_DIM
    hp = d // LANES
    tok = pl.BlockSpec((1, tile, d), lambda i, j: (i, j, 0))
    mod_rows = shift.shape[1]
    mod_tile = tile if mod_rows == s else 1
    mod = pl.BlockSpec((1, mod_tile, d), (lambda i, j: (i, j, 0)) if mod_rows == s else (lambda i, j: (i, 0, 0)))
    in_specs = [tok, mod, mod, _const_spec((1, d)), _const_spec((1, d)), _const_spec(kv_w_main.shape),
                _const_spec(w_f.shape), _const_spec(f_b.shape), _const_spec(b_in_w.shape)]
    args = [x, shift, scale, kv_g, pre_g, kv_w_main, w_f, f_b, b_in_w]
    scratch = []
    if prompt:
        utri = jnp.asarray(np.triu(np.ones((tile, tile), np.float32)), BF16)
        in_specs.append(_const_spec(utri.shape))
        args.append(utri)
        pair = pl.BlockSpec((1, hp, tile, LANES), lambda i, j: (i, 0, j, 0))
        pair_t = pl.BlockSpec((1, hp, LANES, tile), lambda i, j: (i, 0, 0, j))
        out_specs = [tok, pl.BlockSpec((1, d, tile), lambda i, j: (i, 0, j)), tok, pair, pair_t, pair_t,
                     pl.BlockSpec((1, tile, LANES), lambda i, j: (i, j, 0)),
                     pl.BlockSpec((1, n_heads, tile), lambda i, j: (i, 0, j))]
        out_shape = [jax.ShapeDtypeStruct((b, s, d), F32), jax.ShapeDtypeStruct((b, d, s), F32),
                     jax.ShapeDtypeStruct((b, s, d), BF16), jax.ShapeDtypeStruct((b, hp, s, LANES), BF16),
                     jax.ShapeDtypeStruct((b, hp, LANES, s), BF16), jax.ShapeDtypeStruct((b, hp, LANES, s), BF16),
                     jax.ShapeDtypeStruct((b, s, LANES), BF16), jax.ShapeDtypeStruct((b, n_heads, s), F32)]
        scratch = [pltpu.VMEM((n_heads, 1), F32)]
    else:
        out_specs = [tok, tok, tok, pl.BlockSpec((1, tile, n_heads), lambda i, j: (i, j, 0)), tok]
        out_shape = [jax.ShapeDtypeStruct((b, s, d), F32), jax.ShapeDtypeStruct((b, s, d), F32),
                     jax.ShapeDtypeStruct((b, s, d), BF16), jax.ShapeDtypeStruct((b, s, n_heads), F32),
                     jax.ShapeDtypeStruct((b, s, d), F32)]
    return pl.pallas_call(
        functools.partial(_proj_kernel, tile=tile, d=d, prompt=prompt),
        grid=(b, s // tile),
        in_specs=in_specs,
        out_specs=out_specs,
        out_shape=out_shape,
        scratch_shapes=scratch,
        compiler_params=_params("arbitrary", "arbitrary"),
        name="proj_prompt" if prompt else "proj_sample",
    )(*args)


def _prompt_tile(qt_ref, k_ref, f_ref, vt_ref, o_ref, sa_ref, sb_ref, pa_ref, pb_ref, *, tq, n_heads,
                 early_work, late_work):
    tk = tq
    pair = pl.program_id(1)
    qi = pl.program_id(2)
    qt = qt_ref[0, 0]
    row = lax.broadcasted_iota(jnp.int32, (LANES, 1), 0)
    zero = jnp.zeros_like(qt)
    qat = []
    for hd in range(2):
        head = 2 * pair + hd
        mine = (row >= hd * HEAD_DIM) & (row < (hd + 1) * HEAD_DIM)
        pick = (row == head) | (row == n_heads + head) | (row == 2 * n_heads + head)
        minus = jnp.where(pick, -1.0, 0.0).astype(BF16)
        qat.append(jnp.concatenate([jnp.where(mine, qt, zero), jnp.broadcast_to(minus, (LANES, tq))], axis=0))

    def qk(tile, s_ref):
        k0 = pl.multiple_of(tile * tk, tk)
        ka = jnp.concatenate([k_ref[0, 0, pl.ds(k0, tk), :], f_ref[0, pl.ds(k0, tk), :]], axis=1)
        for hd in range(2):
            s_ref[hd] = _dot(ka, qat[hd])

    half = tq // 2

    def qk_diagonal(tile, s_ref):
        k0 = pl.multiple_of(tile * tk, tk)
        ka = jnp.concatenate([k_ref[0, 0, pl.ds(k0, tk), :], f_ref[0, pl.ds(k0, tk), :]], axis=1)
        for hd in range(2):
            s_ref[hd, :half, :] = _dot(ka[:half], qat[hd])
            s_ref[hd, half:, half:] = _dot(ka[half:], qat[hd][:, half:])

    ones = jnp.ones((8, tk), BF16)

    def pv(tile, p_ref, diagonal=False):
        k0 = pl.multiple_of(tile * tk, tk)
        vt = vt_ref[0, 0, :, pl.ds(k0, tk)]
        parts = []
        for hd in range(2):
            va = jnp.concatenate([vt[hd * HEAD_DIM:(hd + 1) * HEAD_DIM], ones], axis=0)
            if diagonal:
                early = _dot(va[:, :half], p_ref[hd, :half, :])
                late = _dot(va[:, half:], p_ref[hd, half:, half:])
                parts.append(jnp.concatenate([early[:, :half], early[:, half:] + late], axis=1))
            else:
                parts.append(_dot(va, p_ref[hd]))
        return parts

    def softmax(s_ref, p_ref, stats, diagonal=False):
        new_stats, alphas = [], []

        def scores(hd, r0):
            c0 = half if diagonal and r0 >= half else 0
            st = s_ref[hd, r0:r0 + STRIP, c0:]
            if diagonal:
                key = lax.broadcasted_iota(jnp.int32, st.shape, 0) + r0
                qry = lax.broadcasted_iota(jnp.int32, st.shape, 1) + c0
                st = jnp.where(key <= qry, st, NEG)
            return st, c0

        def fold(x, op):
            return op(x.reshape(STRIP // 8, 8, x.shape[1]), axis=0)

        for hd in range(2):
            m = stats[hd]
            top = jnp.full((8, tq), NEG, F32)
            for r0 in range(0, tk, STRIP):
                st, c0 = scores(hd, r0)
                upd = jnp.maximum(top[:, c0:], fold(st, jnp.max))
                top = upd if c0 == 0 else jnp.concatenate([top[:, :c0], upd], axis=1)
            m_new = jnp.maximum(m, jnp.max(top, axis=0, keepdims=True))
            alphas.append(jnp.exp2(m - m_new))
            for r0 in range(0, tk, STRIP):
                st, c0 = scores(hd, r0)
                p_ref[hd, r0:r0 + STRIP, c0:] = jnp.exp2((st - m_new[:, c0:]).astype(BF16))
            new_stats.append(m_new)
        return new_stats, alphas

    def rescale_add(alpha, acc, part):
        return [alpha[hd] * acc[hd] + part[hd] for hd in range(2)]

    def pv_before(a):
        return [jnp.where(a > 0, part, 0.0) for part in pv(jnp.maximum(a - 1, 0), pb_ref)]

    @pl.when((pl.program_id(0) == 0) & (pair == 0) & (qi == 0))
    def _():
        pb_ref[...] = jnp.zeros_like(pb_ref)

    early_work()
    qk(0, sa_ref)
    stats = [jnp.full((1, tq), NEG, F32) for _ in range(2)]
    alpha_b = [jnp.ones((1, tq), F32) for _ in range(2)]
    acc = [jnp.zeros((HEAD_DIM + 8, tq), F32) for _ in range(2)]

    def body(t, carry):
        stats, alpha_b, acc = carry
        a = 2 * t
        part = pv_before(a)
        qk(a + 1, sb_ref)
        stats, alpha_a = softmax(sa_ref, pa_ref, stats)
        acc = rescale_add(alpha_b, acc, part)
        part = pv(a, pa_ref)
        qk(a + 2, sa_ref)
        stats, alpha_b = softmax(sb_ref, pb_ref, stats)
        acc = rescale_add(alpha_a, acc, part)
        return stats, alpha_b, acc

    pairs = qi // 2
    stats, alpha_b, acc = lax.fori_loop(0, pairs, body, (stats, alpha_b, acc))
    a = 2 * pairs

    def finish(acc):
        o_ref[0] = jnp.concatenate(
            [acc[hd][:HEAD_DIM] / acc[hd][HEAD_DIM:HEAD_DIM + 1] for hd in range(2)], axis=0).T.astype(o_ref.dtype)

    @pl.when(a == qi)
    def _():
        part = pv_before(a)
        late_work()
        _, alpha_a = softmax(sa_ref, pa_ref, stats, diagonal=True)
        out = rescale_add(alpha_b, acc, part)
        finish(rescale_add(alpha_a, out, pv(a, pa_ref, diagonal=True)))

    @pl.when(a != qi)
    def _():
        part = pv_before(a)
        qk_diagonal(a + 1, sb_ref)
        late_work()
        mid, alpha_a = softmax(sa_ref, pa_ref, stats)
        out = rescale_add(alpha_b, acc, part)
        part = pv(a, pa_ref)
        _, alpha_d = softmax(sb_ref, pb_ref, mid, diagonal=True)
        out = rescale_add(alpha_a, out, part)
        finish(rescale_add(alpha_d, out, pv(a + 1, pb_ref, diagonal=True)))


def _forget_prefix(blocks, carry):
    n_heads, page = blocks[0].shape
    stacked = jnp.concatenate(blocks, axis=0)
    r = lax.broadcasted_iota(jnp.int32, (page, page), 0)
    c = lax.broadcasted_iota(jnp.int32, (page, page), 1)
    within = _dot_exact_lhs(stacked, jnp.where(r <= c, 1.0, 0.0).astype(BF16))
    sums = []
    for b in range(len(blocks)):
        blk = within[b * n_heads:(b + 1) * n_heads] + carry
        sums.append(blk)
        carry = blk[:, page - 1:page]
    return sums, carry


def _sample_chunk(chunk, last, very_first, q_ref, knew_ref, vnew_ref, lfnew_ref, k_pages, v_pages, f_pages,
                  o_ref, qbd_ref, st_ref, acc_ref, m_ref, l_ref, fc_ref, knew_buf, vnew_buf, lfnew_buf,
                  *, steps, d):
    n_heads = d // HEAD_DIM
    page = k_pages[0].shape[-1]
    rows = n_heads * steps
    assert page == LANES and rows % 8 == 0 and 3 * n_heads <= LANES

    row_id = lax.broadcasted_iota(jnp.int32, (rows, 1), 0)
    row_head = _div(row_id, steps)
    row_step = _mod(row_id, steps)
    lane_head = _div(lax.broadcasted_iota(jnp.int32, (1, d), 1), HEAD_DIM)

    @pl.when(very_first)
    def _():
        knew_buf[...] = jnp.zeros_like(knew_buf)
        vnew_buf[...] = jnp.zeros_like(vnew_buf)
        lfnew_buf[...] = jnp.zeros_like(lfnew_buf)

    @pl.when(chunk == 0)
    def _():
        q = q_ref[0]
        qbd = jnp.zeros((rows, d), F32)
        for i in range(steps):
            qbd = jnp.where((lane_head == row_head) & (row_step == i), q[i:i + 1, :], qbd)
        lane = lax.broadcasted_iota(jnp.int32, (1, LANES), 1)
        pick = (lane == row_head) | (lane == n_heads + row_head) | (lane == 2 * n_heads + row_head)
        qbd_ref[...] = jnp.concatenate([qbd, jnp.where(pick, -1.0, 0.0)], axis=1)
        m_ref[...] = jnp.full_like(m_ref, NEG)
        l_ref[...] = jnp.zeros_like(l_ref)
        acc_ref[...] = jnp.zeros_like(acc_ref)
        fc_ref[...] = jnp.zeros_like(fc_ref)

    def online_update(st, value_dot):
        m = m_ref[...]
        m_new = jnp.maximum(m, jnp.max(st, axis=1, keepdims=True))
        alpha = jnp.exp(m - m_new)
        p = jnp.exp(st - m_new)
        l_ref[...] = alpha * l_ref[...] + jnp.sum(p, axis=1, keepdims=True)
        m_ref[...] = m_new
        acc_ref[...] = alpha * acc_ref[...] + value_dot(p)

    def scores():
        sums, carry = _forget_prefix([r[...] for r in f_pages], fc_ref[...])
        fc_ref[...] = carry
        hi, mid, lo = _split3(jnp.concatenate(sums, axis=1))
        keys = hi.shape[1]
        terms = jnp.concatenate([hi.astype(F32), mid.astype(F32), lo.astype(F32),
                                 jnp.zeros((LANES - 3 * n_heads, keys), F32)], axis=0)
        kt = jnp.concatenate([jnp.concatenate([r[...] for r in k_pages], axis=1), terms], axis=0)
        st_ref[...] = _dot(qbd_ref[...], kt)

    def update():
        vt = jnp.concatenate([r[...] for r in v_pages], axis=1)
        online_update(st_ref[...], lambda p: _dot_nt(p, vt))

    def finish():
        @pl.when(chunk == last)
        def _():
            knew_buf[0:steps, :] = knew_ref[0]
            vnew_buf[0:steps, :] = vnew_ref[0]
            lfnew_buf[0:steps, 0:n_heads] = lfnew_ref[0]
            new_sums, _ = _forget_prefix([lfnew_buf[...].T[:n_heads]], fc_ref[...])
            expand = jnp.where(lax.broadcasted_iota(jnp.int32, (rows, n_heads), 1) == row_head,
                               1.0, 0.0).astype(BF16)
            st = _dot_nt(qbd_ref[:, :d], knew_buf[...]) - _dot_exact_rhs(expand, new_sums[0])
            key = lax.broadcasted_iota(jnp.int32, (1, LANES), 1)
            online_update(jnp.where(key <= row_step, st, NEG), lambda p: _dot(p, vnew_buf[...]))

            res = jnp.where(lane_head == row_head, acc_ref[...] / l_ref[...], 0.0)
            gather = jnp.where(lax.broadcasted_iota(jnp.int32, (8, rows), 0) ==
                               _mod(lax.broadcasted_iota(jnp.int32, (8, rows), 1), steps), 1.0, 0.0).astype(BF16)
            o_ref[0] = _dot_exact_rhs(gather, res)[:steps]

    return scores, update, finish


def _attn_kernel(pt_ref, qt_ref, k_ref, f_ref, vt_ref, qs_ref, knew_ref, vnew_ref, lfnew_ref, *rest,
                 tq, n_heads, pages, steps, d):
    (ck_hbm, cv_hbm, cf_hbm, o_ref, os_ref, sa_ref, sb_ref, pa_ref, pb_ref,
     qbd_ref, st_ref, acc_ref, m_ref, l_ref, fc_ref, knew_buf, vnew_buf, lfnew_buf,
     kbuf, vbuf, fbuf, sems) = rest
    chunk = pl.program_id(2)
    n_chunks = pl.num_programs(2)
    step = (pl.program_id(0) * pl.num_programs(1) + pl.program_id(1)) * n_chunks + chunk
    n_steps = pl.num_programs(0) * pl.num_programs(1) * n_chunks
    slot = lax.rem(step, 2)

    def page_copies(of_step, into):
        copies = []
        for r in range(pages):
            pg = pt_ref[of_step * pages + r]
            copies += [pltpu.make_async_copy(ck_hbm.at[pg], kbuf.at[into, r], sems.at[into, 0]),
                       pltpu.make_async_copy(cv_hbm.at[pg], vbuf.at[into, r], sems.at[into, 1]),
                       pltpu.make_async_copy(cf_hbm.at[pg], fbuf.at[into, r], sems.at[into, 2])]
        return copies

    @pl.when(step == 0)
    def _():
        for c in page_copies(0, 0):
            c.start()

    def page_sync():
        for c in page_copies(step, slot):
            c.wait()
        for c in page_copies(jnp.where(step + 1 < n_steps, step + 1, 0), 1 - slot):
            c.start()

    k_pages = [kbuf.at[slot, r] for r in range(pages)]
    v_pages = [vbuf.at[slot, r] for r in range(pages)]
    f_pages = [fbuf.at[slot, r] for r in range(pages)]
    very_first = step == 0
    sample_scores, sample_update, sample_finish = _sample_chunk(
        chunk, n_chunks - 1, very_first, qs_ref, knew_ref, vnew_ref, lfnew_ref,
        k_pages, v_pages, f_pages, os_ref, qbd_ref, st_ref, acc_ref, m_ref, l_ref, fc_ref,
        knew_buf, vnew_buf, lfnew_buf, steps=steps, d=d)

    def early_work():
        page_sync()
        sample_scores()

    _prompt_tile(qt_ref, k_ref, f_ref, vt_ref, o_ref, sa_ref, sb_ref, pa_ref, pb_ref, tq=tq, n_heads=n_heads,
                 early_work=early_work, late_work=sample_update)
    sample_finish()

    @pl.when(step == n_steps - 1)
    def _():
        for c in page_copies(0, 1 - slot):
            c.wait()


def _attention(qtb, kb, fb16, vtb, page_table, q_s, k_new, v_new, lf_new, cache_k, cache_v, cache_logf,
               *, n_heads, tq=512):
    b, hp, s, _ = kb.shape
    nb, steps, d = q_s.shape
    n_pages = page_table.shape[1]
    n_phys, page, _ = cache_logf.shape
    rows = n_heads * steps
    n_tiles = s // tq
    assert nb == b * hp and n_pages % n_tiles == 0
    pages = n_pages // n_tiles
    cache_k = jnp.transpose(cache_k, (0, 2, 3, 1)).reshape(n_phys, d, page)
    cache_v = jnp.transpose(cache_v, (0, 2, 3, 1)).reshape(n_phys, d, page)
    cache_logf = jnp.transpose(cache_logf, (0, 2, 1))

    batch = lambda i, p, j, pt: (i * hp + p, 0, 0)
    in_hbm = pl.BlockSpec(memory_space=pl.ANY)
    grid_spec = pltpu.PrefetchScalarGridSpec(
        num_scalar_prefetch=1,
        grid=(b, hp, n_tiles),
        in_specs=[pl.BlockSpec((1, 1, LANES, tq), lambda i, p, j, pt: (i, p, 0, j)),
                  pl.BlockSpec((1, 1, s, LANES), lambda i, p, j, pt: (i, p, 0, 0)),
                  pl.BlockSpec((1, s, LANES), lambda i, p, j, pt: (i, 0, 0)),
                  pl.BlockSpec((1, 1, LANES, s), lambda i, p, j, pt: (i, p, 0, 0)),
                  pl.BlockSpec((1, steps, d), batch), pl.BlockSpec((1, steps, d), batch),
                  pl.BlockSpec((1, steps, d), batch), pl.BlockSpec((1, steps, n_heads), batch),
                  in_hbm, in_hbm, in_hbm],
        out_specs=[pl.BlockSpec((1, tq, LANES), lambda i, p, j, pt: (i, j, p)),
                   pl.BlockSpec((1, steps, d), batch)],
        scratch_shapes=[pltpu.VMEM((2, tq, tq), F32), pltpu.VMEM((2, tq, tq), F32),
                        pltpu.VMEM((2, tq, tq), BF16), pltpu.VMEM((2, tq, tq), BF16),
                        pltpu.VMEM((rows, d + LANES), F32), pltpu.VMEM((rows, pages * page), F32),
                        pltpu.VMEM((rows, d), F32),
                        pltpu.VMEM((rows, 1), F32), pltpu.VMEM((rows, 1), F32),
                        pltpu.VMEM((n_heads, 1), F32), pltpu.VMEM((LANES, d), F32),
                        pltpu.VMEM((LANES, d), F32), pltpu.VMEM((LANES, LANES), F32),
                        pltpu.VMEM((2, pages, d, page), F32), pltpu.VMEM((2, pages, d, page), F32),
                        pltpu.VMEM((2, pages, n_heads, page), F32), pltpu.SemaphoreType.DMA((2, 3))],
    )
    return pl.pallas_call(
        functools.partial(_attn_kernel, tq=tq, n_heads=n_heads, pages=pages, steps=steps, d=d),
        grid_spec=grid_spec,
        out_shape=[jax.ShapeDtypeStruct((b, s, hp * LANES), BF16), jax.ShapeDtypeStruct((nb, steps, d), F32)],
        compiler_params=_params("arbitrary", "arbitrary", "arbitrary"),
        name="attention",
    )(page_table.reshape(-1), qtb, kb, fb16, vtb, q_s, k_new, v_new, lf_new, cache_k, cache_v, cache_logf)


def _outproj_kernel(o_ref, sz_ref, w_ref, postg_ref, gate_ref, x_ref, y_ref):
    g = (o_ref[0].astype(F32) * sz_ref[0].astype(F32)).astype(BF16)
    out = _dot(g, w_ref[...])
    y_ref[0] = x_ref[0] + gate_ref[0] * (_rms(out) * postg_ref[...])


def _outproj(o, sz, w, post_g, gate, x, *, tile, name):
    b, s, d = x.shape
    tok = pl.BlockSpec((1, tile, d), lambda i, j: (i, j, 0))
    per_row = gate.shape[1] == s
    mod = pl.BlockSpec((1, tile if per_row else 1, d),
                       (lambda i, j: (i, j, 0)) if per_row else (lambda i, j: (i, 0, 0)))
    return pl.pallas_call(
        _outproj_kernel,
        grid=(b, s // tile),
        in_specs=[tok, tok, _const_spec(w.shape), _const_spec((1, d)), mod, tok],
        out_specs=tok,
        out_shape=jax.ShapeDtypeStruct((b, s, d), F32),
        compiler_params=_params("arbitrary", "arbitrary"),
        name=name,
    )(o, sz, w, post_g, gate, x)


def kernel(x_prompt, x_sample, state_pool, cache_k, cache_v, cache_logf, page_table, c_prompt, c_sample,
           ada_w, ada_b, pre_g, post_g, a_in_w, a_grp_w, a_scale, a_out_w, kv_g, kv_w, f_b, b_in_w, b_out_w):
    bp, seq, d = x_prompt.shape
    bs, steps, _ = x_sample.shape
    n_heads = d // HEAD_DIM
    w_a = a_out_w.shape[1]
    rows_s = bs * steps

    pad = (-(bp + bs)) % 8
    c_all = jnp.concatenate([c_prompt, c_sample, jnp.zeros((pad, d), F32)], axis=0)
    mod = _adaln(c_all, ada_w, ada_b)

    def mods(layer):
        m = mod[layer]
        parts = [m[:, i * d:(i + 1) * d] for i in range(3)]
        prompt = [p[:bp].reshape(bp, 1, d) for p in parts]
        sample = [jnp.repeat(p[bp:bp + bs], steps, axis=0) for p in parts]
        return prompt, sample

    (shift0_p, scale0_p, gate0_p), (shift0_s, scale0_s, gate0_s) = mods(0)
    (shift1_p, scale1_p, gate1_p), (shift1_s, scale1_s, gate1_s) = mods(1)

    in_w = a_in_w[0].astype(BF16)
    grp_w = a_grp_w[0].astype(BF16)
    out_w = a_out_w[0].astype(BF16)
    kv_wt = kv_w.T
    kv_w_main = kv_wt[:2 * d].astype(BF16)
    w_f = jnp.pad(kv_wt[2 * d:], ((0, LANES - n_heads), (0, 0))).astype(BF16)
    f_b_pad = jnp.pad(f_b, (0, LANES - n_heads)).reshape(1, LANES)
    bin_w = b_in_w[0].astype(BF16)
    bout_w = b_out_w[0].astype(BF16)
    pre0, pre1 = pre_g[0].reshape(1, d), pre_g[1].reshape(1, d)
    post0, post1 = post_g[0].reshape(1, d), post_g[1].reshape(1, d)
    kvg = kv_g.reshape(1, d)
    asc = a_scale[0].reshape(1, w_a)

    hist = jnp.transpose(state_pool[0], (1, 0, 2)).reshape(POOL_BUF * bs, w_a)
    x1_s3, u_s, k_s, v_s, sz_s, logf_s, q_s = _sample_front(
        (x_sample.reshape(rows_s, d), shift0_s, scale0_s, gate0_s, pre0, post0, in_w, hist, grp_w, asc, out_w),
        (shift1_s[None], scale1_s[None], kvg, pre1, kv_w_main, w_f, f_b_pad, bin_w), steps=steps)
    pool_sample = jnp.concatenate([state_pool[:, :, steps:], u_s.reshape(1, bs, steps, w_a)], axis=2)

    x1_p, tail_p = _pool_prompt(x_prompt, shift0_p, scale0_p, gate0_p, pre0, post0, in_w, grp_w, asc, out_w)
    pool_prompt = tail_p[None, :, HALO - POOL_BUF:, :]
    k_p, vt_p, sz_p, kb, vtb, qtb, fb16, lft_p = _proj(
        x1_p, shift1_p, scale1_p, kvg, pre1, kv_w_main, w_f, f_b_pad, bin_w, tile=512, prompt=True)
    logf_p = jnp.transpose(lft_p, (0, 2, 1))
    v_p = jnp.transpose(vt_p.reshape(bp, n_heads, HEAD_DIM, seq), (0, 3, 1, 2))

    o_p, o_s = _attention(qtb, kb, fb16, vtb, page_table, q_s.reshape(bs, steps, d), k_s.reshape(bs, steps, d),
                          v_s.reshape(bs, steps, d), logf_s.reshape(bs, steps, n_heads),
                          cache_k, cache_v, cache_logf, n_heads=n_heads)
    y_prompt = _outproj(o_p, sz_p, bout_w, post1, gate1_p, x1_p, tile=1024, name="outproj_prompt")
    y_sample = _outproj(o_s.reshape(1, rows_s, d), sz_s, bout_w, post1, gate1_s[None], x1_s3,
                        tile=rows_s, name="outproj_sample")

    return (y_prompt, y_sample.reshape(bs, steps, d), pool_prompt, pool_sample,
            k_p.reshape(bp, seq, n_heads, HEAD_DIM), v_p, logf_p,
            k_s.reshape(bs, steps, n_heads, HEAD_DIM), v_s.reshape(bs, steps, n_heads, HEAD_DIM),
            logf_s.reshape(bs, steps, n_heads))
```
